```python
import jax
import jax.numpy as jnp
from jax import lax
import numpy as np

D_MODEL = 4096
BATCH = 8
SEQ = 2048
DEPTH = 4

MEM_TOKENS = 256
SWA_HEADS = 32
SWA_KV_HEADS = 4
SWA_HEAD_DIM = 64
SWA_WINDOW = 128
SWA_BLOCK = 128
CONV_WIDTH = 2048
CONV_KERNEL = 31
X_HEADS = 4
X_HEAD_DIM = 512
N_BRANCHES = 3
D_FF = 6144
FFN_CONV_KERNEL = 3
NORM_EPS = 1e-6
LN_EPS = 1e-5
NEG_INF = -1e30

SWA_Q = SWA_HEADS * SWA_HEAD_DIM
SWA_KV = SWA_KV_HEADS * SWA_HEAD_DIM
X_Q = X_HEADS * X_HEAD_DIM
IN_SPLITS = (SWA_Q, SWA_KV, SWA_KV, 2 * CONV_WIDTH, X_Q, N_BRANCHES * D_MODEL)
IN_COLS = SWA_Q + 2 * SWA_KV + 2 * CONV_WIDTH + X_Q + N_BRANCHES * D_MODEL

kernel_name = 'hybrid_swa_conformer_xattn_convffn'


def split_sizes(z, sizes):
    out, start = [], 0
    for n in sizes:
        out.append(z[..., start:start + n])
        start += n
    return out


def rmsnorm(x, g):
    xf = x.astype(jnp.float32)
    y = xf * lax.rsqrt(jnp.mean(xf * xf, axis=-1, keepdims=True) + NORM_EPS)
    return (y * g.astype(jnp.float32)).astype(x.dtype)


def layernorm(x, g, b):
    xf = x.astype(jnp.float32)
    mu = jnp.mean(xf, axis=-1, keepdims=True)
    xc = xf - mu
    var = jnp.mean(xc * xc, axis=-1, keepdims=True)
    y = xc * lax.rsqrt(var + LN_EPS) * g.astype(jnp.float32) + b.astype(jnp.float32)
    return y.astype(x.dtype)


def causal_dwconv(x, w, b):
    k = w.shape[0]
    out = lax.conv_general_dilated(
        x, w[:, None, :].astype(x.dtype), window_strides=(1,), padding=[(k - 1, 0)],
        dimension_numbers=('NWC', 'WIO', 'NWC'), feature_group_count=x.shape[-1])
    return out + b.astype(x.dtype)


def sliding_window_attention(q, k, v, sinks):
    b, s, hq, hd = q.shape
    hkv = k.shape[2]
    grp = hq // hkv
    nb = s // SWA_BLOCK
    qb = q.reshape(b, nb, SWA_BLOCK, hkv, grp, hd)
    kb = k.reshape(b, nb, SWA_BLOCK, hkv, hd)
    vb = v.reshape(b, nb, SWA_BLOCK, hkv, hd)
    prev = lambda t: jnp.concatenate([jnp.zeros_like(t[:, :1]), t[:, :-1]], axis=1)
    kw = jnp.concatenate([prev(kb), kb], axis=2)
    vw = jnp.concatenate([prev(vb), vb], axis=2)
    scores = jnp.einsum('bnqhgd,bnkhd->bnhgqk', qb, kw).astype(jnp.float32) * (hd ** -0.5)
    qi = jnp.arange(SWA_BLOCK)[None, :, None]
    kj = jnp.arange(2 * SWA_BLOCK)[None, None, :]
    blk = jnp.arange(nb)[:, None, None]
    delta = qi + SWA_BLOCK - kj
    kpos = blk * SWA_BLOCK + kj - SWA_BLOCK
    valid = (delta >= 0) & (delta < SWA_WINDOW) & (kpos >= 0)
    scores = jnp.where(valid[None, :, None, None], scores, NEG_INF)
    sink = jnp.broadcast_to(sinks.astype(jnp.float32).reshape(1, 1, hkv, grp, 1, 1), scores.shape[:-1] + (1,))
    probs = jax.nn.softmax(jnp.concatenate([scores, sink], axis=-1), axis=-1)[..., :-1]
    out = jnp.einsum('bnhgqk,bnkhd->bnqhgd', probs.astype(v.dtype), vw)
    return out.reshape(b, s, hq * hd)


def cross_attention(q, k, v):
    b, s, h, hd = q.shape
    scores = jnp.einsum('bshd,bmhd->bhsm', q, k).astype(jnp.float32) * (hd ** -0.5)
    probs = jax.nn.softmax(scores, axis=-1).astype(v.dtype)
    return jnp.einsum('bhsm,bmhd->bshd', probs, v).reshape(b, s, h * hd)


def setup_inputs(seed: int = 0) -> dict:
    key = jax.random.key(seed)
    ks = iter(jax.random.split(key, 32))
    L, D = DEPTH, D_MODEL
    res = (2.0 * DEPTH) ** -0.5

    def nrm(shape, scale):
        return jax.random.normal(next(ks), shape, jnp.float32) * scale

    def gain(shape):
        return 1.0 + nrm(shape, 0.02)

    return {
        'x': nrm((BATCH, SEQ, D), 1.0),
        'mem': nrm((BATCH, MEM_TOKENS, D), 1.0),
        'attn_norm': gain((L, D)),
        'w_in': nrm((L, D, IN_COLS), D ** -0.5),
        'swa_q_norm': gain((L, SWA_HEAD_DIM)),
        'swa_k_norm': gain((L, SWA_HEAD_DIM)),
        'swa_sinks': nrm((L, SWA_HEADS), 0.5),
        'conv_dw_w': nrm((L, CONV_KERNEL, CONV_WIDTH), CONV_KERNEL ** -0.5),
        'conv_dw_b': nrm((L, CONV_WIDTH), 0.02),
        'conv_ln_g': gain((L, CONV_WIDTH)),
        'conv_ln_b': nrm((L, CONV_WIDTH), 0.02),
        'mem_norm': gain((L, D)),
        'w_mem_kv': nrm((L, D, 2 * X_Q), D ** -0.5),
        'x_q_norm': gain((L, X_HEAD_DIM)),
        'x_k_norm': gain((L, X_HEAD_DIM)),
        'gate_b': nrm((L, N_BRANCHES * D), 0.02),
        'w_proj_a': nrm((L, SWA_Q, D), SWA_Q ** -0.5),
        'w_proj_c': nrm((L, CONV_WIDTH, D), CONV_WIDTH ** -0.5),
        'w_proj_x': nrm((L, X_Q, D), X_Q ** -0.5),
        'w_o': nrm((L, D, D), D ** -0.5 * res),
        'ffn_norm': gain((L, D)),
        'w_up': nrm((L, D, 2 * D_FF), D ** -0.5),
        'ffn_dw_w': nrm((L, FFN_CONV_KERNEL, 2 * D_FF), FFN_CONV_KERNEL ** -0.5),
        'ffn_dw_b': nrm((L, 2 * D_FF), 0.02),
        'w_down': nrm((L, D_FF, D), D_FF ** -0.5 * res),
    }


def reference(x, mem, attn_norm, w_in, swa_q_norm, swa_k_norm, swa_sinks, conv_dw_w, conv_dw_b,
              conv_ln_g, conv_ln_b, mem_norm, w_mem_kv, x_q_norm, x_k_norm, gate_b, w_proj_a,
              w_proj_c, w_proj_x, w_o, ffn_norm, w_up, ffn_dw_w, ffn_dw_b, w_down):
    b, s, _ = x.shape
    m_len = mem.shape[1]
    for l in range(DEPTH):
        h = rmsnorm(x, attn_norm[l])
        z = h @ w_in[l]
        q, k, v, glu, xq, gates = split_sizes(z, IN_SPLITS)
        q = rmsnorm(q.reshape(b, s, SWA_HEADS, SWA_HEAD_DIM), swa_q_norm[l])
        k = rmsnorm(k.reshape(b, s, SWA_KV_HEADS, SWA_HEAD_DIM), swa_k_norm[l])
        v = v.reshape(b, s, SWA_KV_HEADS, SWA_HEAD_DIM)
        a = sliding_window_attention(q, k, v, swa_sinks[l])
        c = glu[..., :CONV_WIDTH] * jax.nn.sigmoid(glu[..., CONV_WIDTH:])
        c = causal_dwconv(c, conv_dw_w[l], conv_dw_b[l])
        c = jax.nn.silu(layernorm(c, conv_ln_g[l], conv_ln_b[l]))
        mn = rmsnorm(mem, mem_norm[l])
        mkv = mn @ w_mem_kv[l]
        mk = rmsnorm(mkv[..., :X_Q].reshape(b, m_len, X_HEADS, X_HEAD_DIM), x_k_norm[l])
        mv = mkv[..., X_Q:].reshape(b, m_len, X_HEADS, X_HEAD_DIM)
        xq = rmsnorm(xq.reshape(b, s, X_HEADS, X_HEAD_DIM), x_q_norm[l])
        xo = cross_attention(xq, mk, mv)
        g = jax.nn.sigmoid(gates + gate_b[l].astype(gates.dtype))
        ga, gc, gx = split_sizes(g, (D_MODEL, D_MODEL, D_MODEL))
        y = ga * (a @ w_proj_a[l]) + gc * (c @ w_proj_c[l]) + gx * (xo @ w_proj_x[l])
        x = x + y @ w_o[l]
        h = rmsnorm(x, ffn_norm[l])
        u = causal_dwconv(h @ w_up[l], ffn_dw_w[l], ffn_dw_b[l])
        x = x + (jax.nn.silu(u[..., :D_FF]) * u[..., D_FF:]) @ w_down[l]
    return x
```

```python
import functools
import math

import jax
import jax.numpy as jnp
from jax import lax
from jax.experimental import pallas as pl
from jax.experimental.pallas import tpu as pltpu

NORM_EPS = 1e-6
LN_EPS = 1e-5
NEG_INF = -1e30
SWA_BLOCK = 128
CONV_HALO = 32
FFN_HALO = 8
V7X_VMEM_LIMIT_CAP = 56 * 1024 * 1024
LANES = 128

BF16 = jnp.bfloat16
F32 = jnp.float32


def _cparams(n_axes, vmem_bytes):
    limit = min(V7X_VMEM_LIMIT_CAP, max(32 * 1024 * 1024, int(vmem_bytes * 1.25) + (4 << 20)))
    return pltpu.CompilerParams(dimension_semantics=("arbitrary",) * n_axes, vmem_limit_bytes=limit)


def _tile(n, pref, *also):
    g = n
    for a in also:
        g = math.gcd(g, a)
    best = None
    for t in range(min(pref, g), 0, -1):
        if g % t == 0 and (t % LANES == 0 or t == n):
            best = t
            break
    if best is None:
        raise ValueError(f"no tile for {n} {also}")
    return best


def _row_tile(n, pref):
    for t in range(min(pref, n), 0, -1):
        if n % t == 0 and t % 8 == 0:
            return t
    raise ValueError(f"no row tile for {n}")


def _dot(a, b):
    return jnp.dot(a, b, preferred_element_type=F32)


def _sigmoid(x):
    return 1.0 / (1.0 + jnp.exp(-x))


def _rmsnorm_kernel(x_ref, g_ref, o_ref):
    x = x_ref[...]
    ms = jnp.mean(x * x, axis=-1, keepdims=True)
    o_ref[...] = (x * lax.rsqrt(ms + NORM_EPS) * g_ref[...]).astype(o_ref.dtype)


def _rmsnorm(x, g):
    m, d = x.shape
    tr = _row_tile(m, 256)
    return pl.pallas_call(
        _rmsnorm_kernel,
        out_shape=jax.ShapeDtypeStruct((m, d), BF16),
        grid=(m // tr,),
        in_specs=[pl.BlockSpec((tr, d), lambda i: (i, 0)),
                  pl.BlockSpec((1, d), lambda i: (0, 0))],
        out_specs=pl.BlockSpec((tr, d), lambda i: (i, 0)),
        compiler_params=_cparams(1, 2 * tr * d * 6),
        name="rmsnorm",
    )(x, g.reshape(1, d))


def _mm_plain_kernel(a_ref, b_ref, o_ref):
    o_ref[...] = _dot(a_ref[...], b_ref[...]).astype(o_ref.dtype)


def _mm_headnorm_kernel(a_ref, b_ref, g_ref, o_ref):
    z = _dot(a_ref[...], b_ref[...])
    ms = jnp.mean(z * z, axis=-1, keepdims=True)
    o_ref[...] = (z * lax.rsqrt(ms + NORM_EPS) * g_ref[...]).astype(o_ref.dtype)


def _mm_glu_kernel(a_ref, b1_ref, b2_ref, o_ref):
    a = a_ref[...]
    o_ref[...] = (_dot(a, b1_ref[...]) * _sigmoid(_dot(a, b2_ref[...]))).astype(o_ref.dtype)


def _mm_gate_kernel(a_ref, b_ref, bias_ref, o_ref):
    o_ref[...] = _sigmoid(_dot(a_ref[...], b_ref[...]) + bias_ref[...]).astype(o_ref.dtype)


def _mm_resid_kernel(a_ref, b_ref, r_ref, o_ref):
    o_ref[...] = r_ref[...] + _dot(a_ref[...], b_ref[...])


def _matmul(a, w, layer, off, n, *, out_dtype, tm_pref=1024, tn_pref=1024, kind="plain", extra=None,
            off2=None, name="matmul"):
    m, k = a.shape
    offs = [off] + ([off2] if off2 is not None else [])
    if kind == "headnorm":
        tn = extra.shape[-1]
        assert n % tn == 0 and off % tn == 0
    else:
        tn = _tile(n, tn_pref, *offs)
    tm = _row_tile(m, tm_pref)
    out_b = jnp.dtype(out_dtype).itemsize
    n_b = 2 if kind == "glu" else 1
    vmem = 2 * (tm * k * 2 + n_b * k * tn * 2 + tm * tn * out_b) + 3 * tm * tn * 4
    a_spec = pl.BlockSpec((tm, k), lambda i, j: (i, 0))

    def b_spec(o):
        return pl.BlockSpec((None, k, tn), lambda i, j: (layer, 0, j + o // tn))

    o_spec = pl.BlockSpec((tm, tn), lambda i, j: (i, j))
    if kind == "plain":
        kern, ins, specs = _mm_plain_kernel, (a, w), [a_spec, b_spec(off)]
    elif kind == "headnorm":
        kern, ins = _mm_headnorm_kernel, (a, w, extra.reshape(1, tn))
        specs = [a_spec, b_spec(off), pl.BlockSpec((1, tn), lambda i, j: (0, 0))]
    elif kind == "glu":
        kern, ins, specs = _mm_glu_kernel, (a, w, w), [a_spec, b_spec(off), b_spec(off2)]
    elif kind == "gate":
        kern, ins = _mm_gate_kernel, (a, w, extra.reshape(1, n))
        specs = [a_spec, b_spec(off), pl.BlockSpec((1, tn), lambda i, j: (0, j))]
    elif kind == "resid":
        kern, ins = _mm_resid_kernel, (a, w, extra)
        specs = [a_spec, b_spec(off), o_spec]
        vmem += 2 * tm * tn * 4
    else:
        raise ValueError(kind)
    return pl.pallas_call(
        kern,
        out_shape=jax.ShapeDtypeStruct((m, n), out_dtype),
        grid=(m // tm, n // tn),
        in_specs=specs,
        out_specs=o_spec,
        compiler_params=_cparams(2, vmem),
        name=name,
    )(*ins)


def _xattn_kernel(a_ref, b_ref, g_ref, mk_ref, mv_ref, o_ref, *, scale):
    z = _dot(a_ref[...], b_ref[...])
    ms = jnp.mean(z * z, axis=-1, keepdims=True)
    qn = (z * lax.rsqrt(ms + NORM_EPS) * g_ref[...]).astype(BF16)
    s = lax.dot_general(qn, mk_ref[...], (((1,), (1,)), ((), ())), preferred_element_type=F32) * scale
    mx = jnp.max(s, axis=-1, keepdims=True)
    p = jnp.exp(s - mx)
    den = jnp.sum(p, axis=-1, keepdims=True)
    o = _dot(p.astype(BF16), mv_ref[...])
    o_ref[...] = (o / den).astype(o_ref.dtype)


def _xattn(h, w_in, layer, off, xq_gain, mk, mv, seq):
    m, k = h.shape
    bsz, mlen, xq = mk.shape
    hd = xq_gain.shape[-1]
    tm = _row_tile(seq, 1024)
    rows_per_seq = seq // tm
    vmem = 2 * (tm * k * 2 + k * hd * 2 + 2 * mlen * hd * 2 + tm * hd * 2) + 4 * tm * hd * 4 + 3 * tm * mlen * 4
    return pl.pallas_call(
        functools.partial(_xattn_kernel, scale=hd ** -0.5),
        out_shape=jax.ShapeDtypeStruct((m, xq), BF16),
        grid=(m // tm, xq // hd),
        in_specs=[pl.BlockSpec((tm, k), lambda i, j: (i, 0)),
                  pl.BlockSpec((None, k, hd), lambda i, j: (layer, 0, j + off // hd)),
                  pl.BlockSpec((1, hd), lambda i, j: (0, 0)),
                  pl.BlockSpec((None, mlen, hd), lambda i, j: (i // rows_per_seq, 0, j)),
                  pl.BlockSpec((None, mlen, hd), lambda i, j: (i // rows_per_seq, 0, j))],
        out_specs=pl.BlockSpec((tm, hd), lambda i, j: (i, j)),
        compiler_params=_cparams(2, vmem),
        name="xattn",
    )(h, w_in, xq_gain.reshape(1, hd), mk, mv)


def _rms_lastdim(x, g):
    ms = jnp.mean(x * x, axis=-1, keepdims=True)
    return x * lax.rsqrt(ms + NORM_EPS) * g


def _swa_kernel(sink_ref, q_ref, kc_ref, kp_ref, vc_ref, vp_ref, qg_ref, kg_ref, o_ref, *, hq, hkv, hd):
    n = pl.program_id(1)
    blk = SWA_BLOCK
    qi = lax.broadcasted_iota(jnp.int32, (blk, 2 * blk), 0)
    kj = lax.broadcasted_iota(jnp.int32, (blk, 2 * blk), 1)
    delta = qi + blk - kj
    valid = (delta >= 0) & (delta < blk) & ((kj >= blk) | (n > 0))
    scale = hd ** -0.5
    grp = hq // hkv
    qg = qg_ref[...]
    kg = kg_ref[...]
    for h in range(hkv):
        sl = slice(h * hd, (h + 1) * hd)
        kw = jnp.concatenate([kp_ref[:, sl], kc_ref[:, sl]], axis=0)
        kn = _rms_lastdim(kw, kg).astype(BF16)
        vw = jnp.concatenate([vp_ref[:, sl], vc_ref[:, sl]], axis=0).astype(BF16)
        for g in range(grp):
            hh = h * grp + g
            qsl = slice(hh * hd, (hh + 1) * hd)
            qn = _rms_lastdim(q_ref[:, qsl], qg).astype(BF16)
            s = lax.dot_general(qn, kn, (((1,), (1,)), ((), ())), preferred_element_type=F32) * scale
            s = jnp.where(valid, s, NEG_INF)
            sink = sink_ref[hh]
            mx = jnp.maximum(jnp.max(s, axis=-1, keepdims=True), sink)
            p = jnp.exp(s - mx)
            den = jnp.sum(p, axis=-1, keepdims=True) + jnp.exp(sink - mx)
            o = _dot(p.astype(BF16), vw)
            o_ref[:, qsl] = (o / den).astype(o_ref.dtype)


def _swa(qkv, sinks, q_gain, k_gain, *, hq, hkv, hd):
    bsz, seq, _ = qkv.shape
    qw, kvw = hq * hd, hkv * hd
    blk = SWA_BLOCK
    nb = seq // blk
    kb = qw // kvw
    assert qw % kvw == 0 and seq % blk == 0

    def spec(width, col, prev):
        if prev:
            return pl.BlockSpec((None, blk, width), lambda b, n: (b, jnp.maximum(n - 1, 0), col))
        return pl.BlockSpec((None, blk, width), lambda b, n: (b, n, col))

    return pl.pallas_call(
        functools.partial(_swa_kernel, hq=hq, hkv=hkv, hd=hd),
        out_shape=jax.ShapeDtypeStruct((bsz, seq, qw), BF16),
        grid=(bsz, nb),
        in_specs=[pl.BlockSpec(memory_space=pltpu.SMEM),
                  spec(qw, 0, False),
                  spec(kvw, kb, False), spec(kvw, kb, True),
                  spec(kvw, kb + 1, False), spec(kvw, kb + 1, True),
                  pl.BlockSpec((1, hd), lambda b, n: (0, 0)),
                  pl.BlockSpec((1, hd), lambda b, n: (0, 0))],
        out_specs=pl.BlockSpec((None, blk, qw), lambda b, n: (b, n, 0)),
        compiler_params=_cparams(2, 2 * blk * (qw * 6 + 4 * kvw * 4)),
        name="swa",
    )(sinks, qkv, qkv, qkv, qkv, qkv, q_gain.reshape(1, hd), k_gain.reshape(1, hd))


def _dwconv_ln_kernel(cur_ref, halo_ref, w_ref, b_ref, g_ref, beta_ref, o_ref, ext_ref, y_ref, *, ksize):
    i = pl.program_id(1)
    ts, c = cur_ref.shape
    keep = (i > 0).astype(F32)
    ext_ref[0:CONV_HALO, :] = halo_ref[...] * keep
    ext_ref[CONV_HALO:, :] = cur_ref[...]
    base = CONV_HALO - (ksize - 1)

    def chunk(cc, carry):
        c0 = pl.multiple_of(cc * LANES, LANES)
        acc = jnp.broadcast_to(b_ref[:, pl.ds(c0, LANES)], (ts, LANES))
        for k in range(ksize):
            acc = acc + w_ref[k:k + 1, pl.ds(c0, LANES)] * ext_ref[pl.ds(base + k, ts), pl.ds(c0, LANES)]
        y_ref[:, pl.ds(c0, LANES)] = acc
        return carry

    lax.fori_loop(0, c // LANES, chunk, 0)
    y = y_ref[...]
    mu = jnp.mean(y, axis=-1, keepdims=True)
    yc = y - mu
    var = jnp.mean(yc * yc, axis=-1, keepdims=True)
    z = yc * lax.rsqrt(var + LN_EPS) * g_ref[...] + beta_ref[...]
    o_ref[...] = (z * _sigmoid(z)).astype(o_ref.dtype)


def _dwconv_ln(c0, w, b, g, beta):
    bsz, seq, c = c0.shape
    ksize = w.shape[0]
    assert ksize - 1 <= CONV_HALO and c % LANES == 0
    ts = _row_tile(seq, 256)
    assert ts % CONV_HALO == 0
    r = ts // CONV_HALO
    return pl.pallas_call(
        functools.partial(_dwconv_ln_kernel, ksize=ksize),
        out_shape=jax.ShapeDtypeStruct((bsz, seq, c), BF16),
        grid=(bsz, seq // ts),
        in_specs=[pl.BlockSpec((None, ts, c), lambda bb, i: (bb, i, 0)),
                  pl.BlockSpec((None, CONV_HALO, c), lambda bb, i: (bb, jnp.maximum(i * r - 1, 0), 0)),
                  pl.BlockSpec((ksize, c), lambda bb, i: (0, 0)),
                  pl.BlockSpec((1, c), lambda bb, i: (0, 0)),
                  pl.BlockSpec((1, c), lambda bb, i: (0, 0)),
                  pl.BlockSpec((1, c), lambda bb, i: (0, 0))],
        out_specs=pl.BlockSpec((None, ts, c), lambda bb, i: (bb, i, 0)),
        scratch_shapes=[pltpu.VMEM((ts + CONV_HALO, c), F32), pltpu.VMEM((ts, c), F32)],
        compiler_params=_cparams(2, ts * c * 4 * 8),
        name="dwconv_ln",
    )(c0, c0, w, b.reshape(1, c), g.reshape(1, c), beta.reshape(1, c))


def _merge_kernel(a_ref, c_ref, x_ref, wa_ref, wc_ref, wx_ref, ga_ref, gc_ref, gx_ref, o_ref):
    y = ga_ref[...].astype(F32) * _dot(a_ref[...], wa_ref[...])
    y = y + gc_ref[...].astype(F32) * _dot(c_ref[...], wc_ref[...])
    y = y + gx_ref[...].astype(F32) * _dot(x_ref[...], wx_ref[...])
    o_ref[...] = y.astype(o_ref.dtype)


def _merge(a, c, xo, wa, wc, wx, gates, layer, d):
    m = a.shape[0]
    tm = _row_tile(m, 512)
    tn = _tile(d, 512)
    nd = d // tn
    ka, kc, kx = a.shape[1], c.shape[1], xo.shape[1]
    vmem = 2 * (tm * (ka + kc + kx) * 2 + (ka + kc + kx) * tn * 2 + 4 * tm * tn * 2) + 4 * tm * tn * 4

    def a_spec(kk):
        return pl.BlockSpec((tm, kk), lambda i, j: (i, 0))

    def w_spec(kk):
        return pl.BlockSpec((None, kk, tn), lambda i, j: (layer, 0, j))

    def g_spec(which):
        return pl.BlockSpec((tm, tn), lambda i, j: (i, j + which * nd))

    return pl.pallas_call(
        _merge_kernel,
        out_shape=jax.ShapeDtypeStruct((m, d), BF16),
        grid=(m // tm, nd),
        in_specs=[a_spec(ka), a_spec(kc), a_spec(kx), w_spec(ka), w_spec(kc), w_spec(kx),
                  g_spec(0), g_spec(1), g_spec(2)],
        out_specs=pl.BlockSpec((tm, tn), lambda i, j: (i, j)),
        compiler_params=_cparams(2, vmem),
        name="merge",
    )(a, c, xo, wa, wc, wx, gates, gates, gates)


def _ffn_conv_kernel(u1_ref, u2_ref, h1_ref, h2_ref, w1_ref, w2_ref, b1_ref, b2_ref, o_ref, e1_ref, e2_ref, *,
                     ksize):
    i = pl.program_id(1)
    ts = u1_ref.shape[0]
    keep = (i > 0).astype(F32)
    base = FFN_HALO - (ksize - 1)

    def conv(u_ref, h_ref, w_ref, b_ref, e_ref):
        e_ref[0:FFN_HALO, :] = h_ref[...] * keep
        e_ref[FFN_HALO:, :] = u_ref[...]
        acc = jnp.broadcast_to(b_ref[...], u_ref.shape)
        for k in range(ksize):
            acc = acc + w_ref[k:k + 1, :] * e_ref[pl.ds(base + k, ts), :]
        return acc

    c1 = conv(u1_ref, h1_ref, w1_ref, b1_ref, e1_ref)
    c2 = conv(u2_ref, h2_ref, w2_ref, b2_ref, e2_ref)
    o_ref[...] = (c1 * _sigmoid(c1) * c2).astype(o_ref.dtype)


def _ffn_conv(u, w, b, dff):
    bsz, seq, two_f = u.shape
    ksize = w.shape[0]
    assert ksize - 1 <= FFN_HALO
    ts = _row_tile(seq, 512)
    tc = _tile(dff, 512)
    nf = dff // tc
    r = ts // FFN_HALO

    def cur(which):
        return pl.BlockSpec((None, ts, tc), lambda bb, i, j: (bb, i, j + which * nf))

    def halo(which):
        return pl.BlockSpec((None, FFN_HALO, tc), lambda bb, i, j: (bb, jnp.maximum(i * r - 1, 0), j + which * nf))

    def wspec(rows, which):
        return pl.BlockSpec((rows, tc), lambda bb, i, j: (0, j + which * nf))

    return pl.pallas_call(
        functools.partial(_ffn_conv_kernel, ksize=ksize),
        out_shape=jax.ShapeDtypeStruct((bsz, seq, dff), BF16),
        grid=(bsz, seq // ts, nf),
        in_specs=[cur(0), cur(1), halo(0), halo(1), wspec(ksize, 0), wspec(ksize, 1), wspec(1, 0), wspec(1, 1)],
        out_specs=pl.BlockSpec((None, ts, tc), lambda bb, i, j: (bb, i, j)),
        scratch_shapes=[pltpu.VMEM((ts + FFN_HALO, tc), F32), pltpu.VMEM((ts + FFN_HALO, tc), F32)],
        compiler_params=_cparams(3, ts * tc * 4 * 10),
        name="ffn_conv",
    )(u, u, u, u, w, w, b.reshape(1, two_f), b.reshape(1, two_f))


def kernel(x, mem, attn_norm, w_in, swa_q_norm, swa_k_norm, swa_sinks, conv_dw_w, conv_dw_b, conv_ln_g, conv_ln_b,
           mem_norm, w_mem_kv, x_q_norm, x_k_norm, gate_b, w_proj_a, w_proj_c, w_proj_x, w_o, ffn_norm, w_up,
           ffn_dw_w, ffn_dw_b, w_down):
    bsz, seq, d = x.shape
    mlen = mem.shape[1]
    depth = w_in.shape[0]
    hq, hd = swa_sinks.shape[1], swa_q_norm.shape[1]
    swa_q = w_proj_a.shape[1]
    conv_w = conv_dw_w.shape[2]
    x_q = w_mem_kv.shape[2] // 2
    xhd = x_q_norm.shape[1]
    dff = w_down.shape[1]
    swa_kv = (w_in.shape[2] - swa_q - 2 * conv_w - x_q - 3 * d) // 2
    hkv = swa_kv // hd
    assert swa_q == hq * hd and hkv * hd == swa_kv
    off_glu = swa_q + 2 * swa_kv
    off_xq = off_glu + 2 * conv_w
    off_gate = off_xq + x_q
    m = bsz * seq

    w_in_b, w_mem_b = w_in.astype(BF16), w_mem_kv.astype(BF16)
    wa_b, wc_b, wx_b = w_proj_a.astype(BF16), w_proj_c.astype(BF16), w_proj_x.astype(BF16)
    wo_b, wup_b, wdn_b = w_o.astype(BF16), w_up.astype(BF16), w_down.astype(BF16)

    xf = x.reshape(m, d)
    memf = mem.reshape(bsz * mlen, d)
    for l in range(depth):
        h = _rmsnorm(xf, attn_norm[l])
        qkv = _matmul(h, w_in_b, l, 0, off_glu, out_dtype=F32, tn_pref=512, name="inproj_qkv")
        c0 = _matmul(h, w_in_b, l, off_glu, conv_w, off2=off_glu + conv_w, kind="glu", out_dtype=F32,
                     tn_pref=512, name="inproj_glu")
        gates = _matmul(h, w_in_b, l, off_gate, 3 * d, kind="gate", extra=gate_b[l], out_dtype=BF16,
                        name="inproj_gates")
        mn = _rmsnorm(memf, mem_norm[l])
        mk = _matmul(mn, w_mem_b, l, 0, x_q, kind="headnorm", extra=x_k_norm[l], out_dtype=BF16, name="mem_k")
        mv = _matmul(mn, w_mem_b, l, x_q, x_q, out_dtype=BF16, tn_pref=512, name="mem_v")
        xo = _xattn(h, w_in_b, l, off_xq, x_q_norm[l], mk.reshape(bsz, mlen, x_q), mv.reshape(bsz, mlen, x_q), seq)
        a = _swa(qkv.reshape(bsz, seq, off_glu), swa_sinks[l], swa_q_norm[l], swa_k_norm[l], hq=hq, hkv=hkv, hd=hd)
        c = _dwconv_ln(c0.reshape(bsz, seq, conv_w), conv_dw_w[l], conv_dw_b[l], conv_ln_g[l], conv_ln_b[l])
        y = _merge(a.reshape(m, swa_q), c.reshape(m, conv_w), xo, wa_b, wc_b, wx_b, gates, l, d)
        xf = _matmul(y, wo_b, l, 0, d, kind="resid", extra=xf, out_dtype=F32, name="out_proj")
        h2 = _rmsnorm(xf, ffn_norm[l])
        u = _matmul(h2, wup_b, l, 0, 2 * dff, out_dtype=F32, name="ffn_up")
        act = _ffn_conv(u.reshape(bsz, seq, 2 * dff), ffn_dw_w[l], ffn_dw_b[l], dff)
        xf = _matmul(act.reshape(m, dff), wdn_b, l, 0, d, kind="resid", extra=xf, out_dtype=F32, tn_pref=512,
                     name="ffn_down")
    return xf.reshape(bsz, seq, d)
```

```python
import functools
import math

import jax
import jax.numpy as jnp
from jax import lax
from jax.experimental import pallas as pl
from jax.experimental.pallas import tpu as pltpu

NORM_EPS = 1e-6
LN_EPS = 1e-5
NEG_INF = -1e30
SWA_BLOCK = 128
CONV_HALO = 32
FFN_HALO = 8
V7X_VMEM_LIMIT_CAP = 56 * 1024 * 1024
LANES = 128
MXU_K = 256

BF16 = jnp.bfloat16
F32 = jnp.float32


def _cparams(n_axes, vmem_bytes):
    limit = min(V7X_VMEM_LIMIT_CAP, max(32 * 1024 * 1024, int(vmem_bytes * 1.25) + (4 << 20)))
    return pltpu.CompilerParams(dimension_semantics=("arbitrary",) * n_axes, vmem_limit_bytes=limit)


def _tile(n, pref, *also):
    g = n
    for a in also:
        g = math.gcd(g, a)
    best = None
    for t in range(min(pref, g), 0, -1):
        if g % t == 0 and (t % LANES == 0 or t == n):
            best = t
            break
    if best is None:
        raise ValueError(f"no tile for {n} {also}")
    return best


def _row_tile(n, pref):
    for t in range(min(pref, n), 0, -1):
        if n % t == 0 and t % 8 == 0:
            return t
    raise ValueError(f"no row tile for {n}")


def _dot(a, b):
    return jnp.dot(a, b, preferred_element_type=F32)


def _sigmoid(x):
    return 1.0 / (1.0 + jnp.exp(-x))


def _rmsnorm_kernel(x_ref, g_ref, o_ref):
    x = x_ref[...]
    ms = jnp.mean(x * x, axis=-1, keepdims=True)
    o_ref[...] = (x * lax.rsqrt(ms + NORM_EPS) * g_ref[...]).astype(o_ref.dtype)


def _rmsnorm(x, g):
    m, d = x.shape
    tr = _row_tile(m, 256)
    return pl.pallas_call(
        _rmsnorm_kernel,
        out_shape=jax.ShapeDtypeStruct((m, d), BF16),
        grid=(m // tr,),
        in_specs=[pl.BlockSpec((tr, d), lambda i: (i, 0)),
                  pl.BlockSpec((1, d), lambda i: (0, 0))],
        out_specs=pl.BlockSpec((tr, d), lambda i: (i, 0)),
        compiler_params=_cparams(1, 2 * tr * d * 6),
        name="rmsnorm",
    )(x, g.reshape(1, d))


def _mm_plain_kernel(a_ref, b_ref, o_ref):
    o_ref[...] = _dot(a_ref[...], b_ref[...]).astype(o_ref.dtype)


def _mm_headnorm_kernel(a_ref, b_ref, g_ref, o_ref):
    z = _dot(a_ref[...], b_ref[...])
    ms = jnp.mean(z * z, axis=-1, keepdims=True)
    o_ref[...] = (z * lax.rsqrt(ms + NORM_EPS) * g_ref[...]).astype(o_ref.dtype)


def _mm_glu_kernel(a_ref, b1_ref, b2_ref, o_ref):
    a = a_ref[...]
    o_ref[...] = (_dot(a, b1_ref[...]) * _sigmoid(_dot(a, b2_ref[...]))).astype(o_ref.dtype)


def _mm_gate_kernel(a_ref, b_ref, bias_ref, o_ref):
    o_ref[...] = _sigmoid(_dot(a_ref[...], b_ref[...]) + bias_ref[...]).astype(o_ref.dtype)


def _mm_resid_kernel(a_ref, b_ref, r_ref, o_ref):
    o_ref[...] = r_ref[...] + _dot(a_ref[...], b_ref[...])


def _matmul(a, w, layer, off, n, *, out_dtype, tm_pref=1024, tn_pref=1024, kind="plain", extra=None,
            off2=None, name="matmul"):
    m, k = a.shape
    offs = [off] + ([off2] if off2 is not None else [])
    if kind == "headnorm":
        tn = extra.shape[-1]
        assert n % tn == 0 and off % tn == 0
    else:
        tn = _tile(n, tn_pref, *offs)
    tm = _row_tile(m, tm_pref)
    out_b = jnp.dtype(out_dtype).itemsize
    n_b = 2 if kind == "glu" else 1
    vmem = 2 * (tm * k * 2 + n_b * k * tn * 2 + tm * tn * out_b) + 3 * tm * tn * 4
    a_spec = pl.BlockSpec((tm, k), lambda i, j: (i, 0))

    def b_spec(o):
        return pl.BlockSpec((None, k, tn), lambda i, j: (layer, 0, j + o // tn))

    o_spec = pl.BlockSpec((tm, tn), lambda i, j: (i, j))
    if kind == "plain":
        kern, ins, specs = _mm_plain_kernel, (a, w), [a_spec, b_spec(off)]
    elif kind == "headnorm":
        kern, ins = _mm_headnorm_kernel, (a, w, extra.reshape(1, tn))
        specs = [a_spec, b_spec(off), pl.BlockSpec((1, tn), lambda i, j: (0, 0))]
    elif kind == "glu":
        kern, ins, specs = _mm_glu_kernel, (a, w, w), [a_spec, b_spec(off), b_spec(off2)]
    elif kind == "gate":
        kern, ins = _mm_gate_kernel, (a, w, extra.reshape(1, n))
        specs = [a_spec, b_spec(off), pl.BlockSpec((1, tn), lambda i, j: (0, j))]
    elif kind == "resid":
        kern, ins = _mm_resid_kernel, (a, w, extra)
        specs = [a_spec, b_spec(off), o_spec]
        vmem += 2 * tm * tn * 4
    else:
        raise ValueError(kind)
    return pl.pallas_call(
        kern,
        out_shape=jax.ShapeDtypeStruct((m, n), out_dtype),
        grid=(m // tm, n // tn),
        in_specs=specs,
        out_specs=o_spec,
        compiler_params=_cparams(2, vmem),
        name=name,
    )(*ins)


def _xattn_kernel(a_ref, b_ref, g_ref, mk_ref, mv_ref, o_ref, *, scale):
    z = _dot(a_ref[...], b_ref[...])
    ms = jnp.mean(z * z, axis=-1, keepdims=True)
    qn = (z * lax.rsqrt(ms + NORM_EPS) * g_ref[...]).astype(BF16)
    s = lax.dot_general(qn, mk_ref[...], (((1,), (1,)), ((), ())), preferred_element_type=F32) * scale
    mx = jnp.max(s, axis=-1, keepdims=True)
    p = jnp.exp(s - mx)
    den = jnp.sum(p, axis=-1, keepdims=True)
    o = _dot(p.astype(BF16), mv_ref[...])
    o_ref[...] = (o / den).astype(o_ref.dtype)


def _xattn(h, w_in, layer, off, xq_gain, mk, mv, seq):
    m, k = h.shape
    bsz, mlen, xq = mk.shape
    hd = xq_gain.shape[-1]
    tm = _row_tile(seq, 1024)
    rows_per_seq = seq // tm
    vmem = 2 * (tm * k * 2 + k * hd * 2 + 2 * mlen * hd * 2 + tm * hd * 2) + 4 * tm * hd * 4 + 3 * tm * mlen * 4
    return pl.pallas_call(
        functools.partial(_xattn_kernel, scale=hd ** -0.5),
        out_shape=jax.ShapeDtypeStruct((m, xq), BF16),
        grid=(m // tm, xq // hd),
        in_specs=[pl.BlockSpec((tm, k), lambda i, j: (i, 0)),
                  pl.BlockSpec((None, k, hd), lambda i, j: (layer, 0, j + off // hd)),
                  pl.BlockSpec((1, hd), lambda i, j: (0, 0)),
                  pl.BlockSpec((None, mlen, hd), lambda i, j: (i // rows_per_seq, 0, j)),
                  pl.BlockSpec((None, mlen, hd), lambda i, j: (i // rows_per_seq, 0, j))],
        out_specs=pl.BlockSpec((tm, hd), lambda i, j: (i, j)),
        compiler_params=_cparams(2, vmem),
        name="xattn",
    )(h, w_in, xq_gain.reshape(1, hd), mk, mv)


def _pair_rms(x, lo, gain2):
    hd = LANES // 2
    x2 = x * x
    s_lo = jnp.sum(jnp.where(lo, x2, 0.0), axis=-1, keepdims=True)
    s_hi = jnp.sum(jnp.where(lo, 0.0, x2), axis=-1, keepdims=True)
    r = jnp.where(lo, lax.rsqrt(s_lo * (1.0 / hd) + NORM_EPS), lax.rsqrt(s_hi * (1.0 / hd) + NORM_EPS))
    return x * r * gain2


def _both_halves(col, lo, half):
    keep = lo if half == 0 else jnp.logical_not(lo)
    base = jnp.where(keep, col, 0.0)
    other = pltpu.roll(base, LANES // 2, axis=1)
    return jnp.concatenate([base, other] if half == 0 else [other, base], axis=0)


def _swa_kernel(sink_ref, q_ref, kc_ref, kp_ref, vc_ref, vp_ref, qg_ref, kg_ref, o_ref, *, hq, hkv, hd):
    n = pl.program_id(1)
    blk = SWA_BLOCK
    grp = hq // hkv
    tiles = grp // 2
    rows = tiles * blk
    lane = lax.broadcasted_iota(jnp.int32, (1, LANES), 1)
    lo = lane < hd
    qi = lax.broadcasted_iota(jnp.int32, (blk, 2 * blk), 0)
    kj = lax.broadcasted_iota(jnp.int32, (blk, 2 * blk), 1)
    delta = qi + blk - kj
    valid = (delta >= 0) & (delta < blk) & ((kj >= blk) | (n > 0))
    key0 = lax.broadcasted_iota(jnp.int32, (4 * blk, 1), 0) & (2 * blk - 1) == 0
    qg2 = qg_ref[...] * (hd ** -0.5)
    kg2 = kg_ref[...]
    for c2 in range(hkv // 2):
        csl = slice(c2 * LANES, (c2 + 1) * LANES)
        kcol = _pair_rms(jnp.concatenate([kp_ref[:, csl], kc_ref[:, csl]], axis=0), lo, kg2)
        vcol = jnp.concatenate([vp_ref[:, csl], vc_ref[:, csl]], axis=0)
        for half in range(2):
            h = 2 * c2 + half
            kpad = _both_halves(kcol, lo, half).astype(BF16)
            vpad = jnp.where(key0, 0.0, _both_halves(vcol, lo, half)).astype(BF16)
            q4 = jnp.concatenate([q_ref[:, (h * tiles + t) * LANES:(h * tiles + t + 1) * LANES]
                                  for t in range(tiles)], axis=0)
            qn = _pair_rms(q4, lo, qg2).astype(BF16)
            s = lax.dot_general(qn, kpad, (((1,), (1,)), ((), ())), preferred_element_type=F32)
            ps, rds = [], []
            for e in range(2):
                cols = []
                for t in range(tiles):
                    st = jnp.where(valid, s[t * blk:(t + 1) * blk, e * 2 * blk:(e + 1) * 2 * blk], NEG_INF)
                    first = jnp.where(lane == 0, sink_ref[h * grp + 2 * t + e], st[:, :LANES])
                    cols.append(jnp.concatenate([first, st[:, LANES:]], axis=1))
                se = jnp.concatenate(cols, axis=0)
                mx = jnp.max(se, axis=-1, keepdims=True)
                p = jnp.exp(se - mx)
                rds.append(1.0 / jnp.sum(p, axis=-1, keepdims=True))
                ps.append(p.astype(BF16))
            o4 = _dot(jnp.concatenate(ps, axis=1), vpad) * jnp.where(lo, rds[0], rds[1])
            for t in range(tiles):
                o_ref[:, (h * tiles + t) * LANES:(h * tiles + t + 1) * LANES] = (
                    o4[t * blk:(t + 1) * blk, :].astype(o_ref.dtype))


def _swa(qkv, sinks, q_gain, k_gain, *, hq, hkv, hd):
    bsz, seq, _ = qkv.shape
    qw, kvw = hq * hd, hkv * hd
    blk = SWA_BLOCK
    nb = seq // blk
    kb = qw // kvw
    assert qw % kvw == 0 and seq % blk == 0
    assert 2 * hd == LANES and hkv % 2 == 0 and (hq // hkv) % 2 == 0
    q_gain = jnp.concatenate([q_gain, q_gain])
    k_gain = jnp.concatenate([k_gain, k_gain])

    def spec(width, col, prev):
        if prev:
            return pl.BlockSpec((None, blk, width), lambda b, n: (b, jnp.maximum(n - 1, 0), col))
        return pl.BlockSpec((None, blk, width), lambda b, n: (b, n, col))

    return pl.pallas_call(
        functools.partial(_swa_kernel, hq=hq, hkv=hkv, hd=hd),
        out_shape=jax.ShapeDtypeStruct((bsz, seq, qw), BF16),
        grid=(bsz, nb),
        in_specs=[pl.BlockSpec(memory_space=pltpu.SMEM),
                  spec(qw, 0, False),
                  spec(kvw, kb, False), spec(kvw, kb, True),
                  spec(kvw, kb + 1, False), spec(kvw, kb + 1, True),
                  pl.BlockSpec((1, LANES), lambda b, n: (0, 0)),
                  pl.BlockSpec((1, LANES), lambda b, n: (0, 0))],
        out_specs=pl.BlockSpec((None, blk, qw), lambda b, n: (b, n, 0)),
        compiler_params=_cparams(2, 2 * blk * (qw * 6 + 4 * kvw * 4)),
        name="swa",
    )(sinks, qkv, qkv, qkv, qkv, qkv, q_gain.reshape(1, LANES), k_gain.reshape(1, LANES))


def _dwconv_ln_kernel(cur_ref, halo_ref, w_ref, b_ref, g_ref, beta_ref, o_ref, ext_ref, y_ref, sh_ref, *, ksize):
    i = pl.program_id(1)
    ts, c = cur_ref.shape
    keep = (i > 0).astype(F32)
    ext_ref[0:CONV_HALO, :] = halo_ref[...] * keep
    ext_ref[CONV_HALO:, :] = cur_ref[...]
    base = CONV_HALO - (ksize - 1)

    rb = min(ts, 128)

    def chunk(cc, carry):
        c0 = pl.multiple_of(cc * LANES, LANES)
        lanes = pl.ds(c0, LANES)
        nrows = ts + CONV_HALO - 8
        for s in range(1, 8):
            sh_ref[s - 1, 0:nrows, :] = ext_ref[pl.ds(s, nrows), lanes]
        for r0 in range(0, ts, rb):
            acc = jnp.broadcast_to(b_ref[:, lanes], (rb, LANES))
            for k in range(ksize):
                s, a8 = (base + k) % 8, 8 * ((base + k) // 8)
                if s == 0:
                    win = ext_ref[pl.ds(r0 + a8, rb), lanes]
                else:
                    win = sh_ref[s - 1, pl.ds(r0 + a8, rb), :]
                acc = acc + w_ref[k:k + 1, lanes] * win
            y_ref[pl.ds(r0, rb), lanes] = acc
        return carry

    lax.fori_loop(0, c // LANES, chunk, 0)
    y = y_ref[...]
    mu = jnp.mean(y, axis=-1, keepdims=True)
    yc = y - mu
    var = jnp.mean(yc * yc, axis=-1, keepdims=True)
    z = yc * lax.rsqrt(var + LN_EPS) * g_ref[...] + beta_ref[...]
    o_ref[...] = (z * _sigmoid(z)).astype(o_ref.dtype)


def _dwconv_ln(c0, w, b, g, beta):
    bsz, seq, c = c0.shape
    ksize = w.shape[0]
    assert ksize - 1 <= CONV_HALO and c % LANES == 0
    ts = _row_tile(seq, 256)
    assert ts % CONV_HALO == 0
    r = ts // CONV_HALO
    return pl.pallas_call(
        functools.partial(_dwconv_ln_kernel, ksize=ksize),
        out_shape=jax.ShapeDtypeStruct((bsz, seq, c), BF16),
        grid=(bsz, seq // ts),
        in_specs=[pl.BlockSpec((None, ts, c), lambda bb, i: (bb, i, 0)),
                  pl.BlockSpec((None, CONV_HALO, c), lambda bb, i: (bb, jnp.maximum(i * r - 1, 0), 0)),
                  pl.BlockSpec((ksize, c), lambda bb, i: (0, 0)),
                  pl.BlockSpec((1, c), lambda bb, i: (0, 0)),
                  pl.BlockSpec((1, c), lambda bb, i: (0, 0)),
                  pl.BlockSpec((1, c), lambda bb, i: (0, 0))],
        out_specs=pl.BlockSpec((None, ts, c), lambda bb, i: (bb, i, 0)),
        scratch_shapes=[pltpu.VMEM((ts + CONV_HALO, c), F32), pltpu.VMEM((ts, c), F32),
                        pltpu.VMEM((7, ts + CONV_HALO - 8, LANES), F32)],
        compiler_params=_cparams(2, ts * c * 4 * 8),
        name="dwconv_ln",
    )(c0, c0, w, b.reshape(1, c), g.reshape(1, c), beta.reshape(1, c))


def _merge_kernel(a_ref, c_ref, x_ref, wa_ref, wc_ref, wx_ref, ga_ref, gc_ref, gx_ref, o_ref):
    y = ga_ref[...].astype(F32) * _dot(a_ref[...], wa_ref[...])
    y = y + gc_ref[...].astype(F32) * _dot(c_ref[...], wc_ref[...])
    y = y + gx_ref[...].astype(F32) * _dot(x_ref[...], wx_ref[...])
    o_ref[...] = y.astype(o_ref.dtype)


def _merge(a, c, xo, wa, wc, wx, gates, layer, d):
    m = a.shape[0]
    tm = _row_tile(m, 512)
    tn = _tile(d, 512)
    nd = d // tn
    ka, kc, kx = a.shape[1], c.shape[1], xo.shape[1]
    vmem = 2 * (tm * (ka + kc + kx) * 2 + (ka + kc + kx) * tn * 2 + 4 * tm * tn * 2) + 4 * tm * tn * 4

    def a_spec(kk):
        return pl.BlockSpec((tm, kk), lambda i, j: (i, 0))

    def w_spec(kk):
        return pl.BlockSpec((None, kk, tn), lambda i, j: (layer, 0, j))

    def g_spec(which):
        return pl.BlockSpec((tm, tn), lambda i, j: (i, j + which * nd))

    return pl.pallas_call(
        _merge_kernel,
        out_shape=jax.ShapeDtypeStruct((m, d), BF16),
        grid=(m // tm, nd),
        in_specs=[a_spec(ka), a_spec(kc), a_spec(kx), w_spec(ka), w_spec(kc), w_spec(kx),
                  g_spec(0), g_spec(1), g_spec(2)],
        out_specs=pl.BlockSpec((tm, tn), lambda i, j: (i, j)),
        compiler_params=_cparams(2, vmem),
        name="merge",
    )(a, c, xo, wa, wc, wx, gates, gates, gates)


def _ffn_up_kernel(ti_ref, tj_ref, a_ref, b1_ref, b2_ref, w1_ref, w2_ref, c1_ref, c2_ref, o_ref,
                   za1, za2, zb1, zb2, carry1, carry2, *, ksize, tiles_per_seq, row_chunk):
    t = pl.program_id(0)
    tm = a_ref.shape[0]
    halo = FFN_HALO

    @pl.when(t == 0)
    def _():
        def zero_rows(i, carry):
            r = pl.multiple_of(i * halo, halo)
            zb1[pl.ds(r, halo), :] = jnp.zeros((halo, zb1.shape[1]), F32)
            zb2[pl.ds(r, halo), :] = jnp.zeros((halo, zb2.shape[1]), F32)
            return carry

        lax.fori_loop(0, zb1.shape[0] // halo, zero_rows, 0)

    def step(zr1, zr2, zw1, zw2):
        i_prev, j_prev = ti_ref[t], tj_ref[t]
        seq_start = lax.rem(i_prev, tiles_per_seq) == 0

        @pl.when(seq_start)
        def _():
            zr1[0:halo, :] = jnp.zeros((halo, zr1.shape[1]), F32)
            zr2[0:halo, :] = jnp.zeros((halo, zr2.shape[1]), F32)
            carry1[j_prev] = zr1[tm:tm + halo, :]
            carry2[j_prev] = zr2[tm:tm + halo, :]

        @pl.when(jnp.logical_not(seq_start))
        def _():
            zr1[0:halo, :] = carry1[j_prev]
            zr2[0:halo, :] = carry2[j_prev]
            carry1[j_prev] = zr1[tm:tm + halo, :]
            carry2[j_prev] = zr2[tm:tm + halo, :]

        def conv(zr, w_ref, c_ref, r):
            acc = c_ref[...] + w_ref[ksize - 1:ksize, :] * zr[halo + r:halo + r + row_chunk, :]
            for k in range(ksize - 1):
                shift = ksize - 1 - k
                acc = acc + w_ref[k:k + 1, :] * zr[pl.ds(halo + r - shift, row_chunk), :]
            return acc

        def epilogue_rows(r0, r1):
            for r in range(r0, r1, row_chunk):
                u1 = conv(zr1, w1_ref, c1_ref, r)
                u2 = conv(zr2, w2_ref, c2_ref, r)
                o_ref[r:r + row_chunk, :] = (u1 * _sigmoid(u1) * u2).astype(o_ref.dtype)

        kdim = a_ref.shape[1]
        kc = _tile(kdim, MXU_K)
        nk = kdim // kc
        bounds = [(tm * s // (2 * nk)) // row_chunk * row_chunk for s in range(2 * nk + 1)]
        slot = 0
        for zw, b_ref in ((zw1, b1_ref), (zw2, b2_ref)):
            acc = None
            for kk in range(nk):
                part = _dot(a_ref[:, kk * kc:(kk + 1) * kc], b_ref[kk * kc:(kk + 1) * kc, :])
                acc = part if acc is None else acc + part
                epilogue_rows(bounds[slot], bounds[slot + 1])
                slot += 1
            zw[halo:, :] = acc

    parity = lax.rem(t, 2)

    @pl.when(parity == 0)
    def _():
        step(zb1, zb2, za1, za2)

    @pl.when(parity == 1)
    def _():
        step(za1, za2, zb1, zb2)


def _ffn_up(h, w_up, layer, conv_w, conv_b, dff, seq):
    m, k = h.shape
    ksize = conv_w.shape[0]
    assert ksize - 1 <= FFN_HALO
    tm = _row_tile(seq, 1024)
    tn = _tile(dff, 512)
    ni, nf = m // tm, dff // tn
    steps = ni * nf
    order = [min(max(s - 1, 0), steps - 1) for s in range(steps + 2)]
    ti = jnp.asarray([s // nf for s in order], jnp.int32)
    tj = jnp.asarray([s % nf for s in order], jnp.int32)
    vmem = 2 * (tm * k * 2 + 2 * k * tn * 2 + tm * tn * 2) + 4 * (tm + FFN_HALO) * tn * 4 + 2 * nf * FFN_HALO * tn * 4

    def wspec(rows, which):
        return pl.BlockSpec((rows, tn), lambda t, ti, tj: (0, tj[t] + which * nf))

    def bspec(which):
        return pl.BlockSpec((None, k, tn), lambda t, ti, tj: (layer, 0, tj[t + 1] + which * nf))

    zbuf = pltpu.VMEM((tm + FFN_HALO, tn), F32)
    carry = pltpu.VMEM((nf, FFN_HALO, tn), F32)
    two_f = 2 * dff
    return pl.pallas_call(
        functools.partial(_ffn_up_kernel, ksize=ksize, tiles_per_seq=seq // tm, row_chunk=min(tm, 32)),
        out_shape=jax.ShapeDtypeStruct((m, dff), BF16),
        grid_spec=pltpu.PrefetchScalarGridSpec(
            num_scalar_prefetch=2,
            grid=(steps + 1,),
            in_specs=[pl.BlockSpec((tm, k), lambda t, ti, tj: (ti[t + 1], 0)),
                      bspec(0), bspec(1), wspec(ksize, 0), wspec(ksize, 1), wspec(1, 0), wspec(1, 1)],
            out_specs=pl.BlockSpec((tm, tn), lambda t, ti, tj: (ti[t], tj[t])),
            scratch_shapes=[zbuf, zbuf, zbuf, zbuf, carry, carry]),
        compiler_params=_cparams(1, vmem),
        name="ffn_up",
    )(ti, tj, h, w_up, w_up, conv_w, conv_w, conv_b.reshape(1, two_f), conv_b.reshape(1, two_f))


def kernel(x, mem, attn_norm, w_in, swa_q_norm, swa_k_norm, swa_sinks, conv_dw_w, conv_dw_b, conv_ln_g, conv_ln_b,
           mem_norm, w_mem_kv, x_q_norm, x_k_norm, gate_b, w_proj_a, w_proj_c, w_proj_x, w_o, ffn_norm, w_up,
           ffn_dw_w, ffn_dw_b, w_down):
    bsz, seq, d = x.shape
    mlen = mem.shape[1]
    depth = w_in.shape[0]
    hq, hd = swa_sinks.shape[1], swa_q_norm.shape[1]
    swa_q = w_proj_a.shape[1]
    conv_w = conv_dw_w.shape[2]
    x_q = w_mem_kv.shape[2] // 2
    xhd = x_q_norm.shape[1]
    dff = w_down.shape[1]
    swa_kv = (w_in.shape[2] - swa_q - 2 * conv_w - x_q - 3 * d) // 2
    hkv = swa_kv // hd
    assert swa_q == hq * hd and hkv * hd == swa_kv
    off_glu = swa_q + 2 * swa_kv
    off_xq = off_glu + 2 * conv_w
    off_gate = off_xq + x_q
    m = bsz * seq

    w_in_b, w_mem_b = w_in.astype(BF16), w_mem_kv.astype(BF16)
    wa_b, wc_b, wx_b = w_proj_a.astype(BF16), w_proj_c.astype(BF16), w_proj_x.astype(BF16)
    wo_b, wup_b, wdn_b = w_o.astype(BF16), w_up.astype(BF16), w_down.astype(BF16)

    xf = x.reshape(m, d)
    memf = mem.reshape(bsz * mlen, d)
    for l in range(depth):
        h = _rmsnorm(xf, attn_norm[l])
        qkv = _matmul(h, w_in_b, l, 0, off_glu, out_dtype=F32, tn_pref=512, name="inproj_qkv")
        c0 = _matmul(h, w_in_b, l, off_glu, conv_w, off2=off_glu + conv_w, kind="glu", out_dtype=F32,
                     tn_pref=512, name="inproj_glu")
        gates = _matmul(h, w_in_b, l, off_gate, 3 * d, kind="gate", extra=gate_b[l], out_dtype=BF16,
                        name="inproj_gates")
        mn = _rmsnorm(memf, mem_norm[l])
        mk = _matmul(mn, w_mem_b, l, 0, x_q, kind="headnorm", extra=x_k_norm[l], out_dtype=BF16, name="mem_k")
        mv = _matmul(mn, w_mem_b, l, x_q, x_q, out_dtype=BF16, tn_pref=512, name="mem_v")
        xo = _xattn(h, w_in_b, l, off_xq, x_q_norm[l], mk.reshape(bsz, mlen, x_q), mv.reshape(bsz, mlen, x_q), seq)
        a = _swa(qkv.reshape(bsz, seq, off_glu), swa_sinks[l], swa_q_norm[l], swa_k_norm[l], hq=hq, hkv=hkv, hd=hd)
        c = _dwconv_ln(c0.reshape(bsz, seq, conv_w), conv_dw_w[l], conv_dw_b[l], conv_ln_g[l], conv_ln_b[l])
        y = _merge(a.reshape(m, swa_q), c.reshape(m, conv_w), xo, wa_b, wc_b, wx_b, gates, l, d)
        xf = _matmul(y, wo_b, l, 0, d, kind="resid", extra=xf, out_dtype=F32, name="out_proj")
        h2 = _rmsnorm(xf, ffn_norm[l])
        act = _ffn_up(h2, wup_b, l, ffn_dw_w[l], ffn_dw_b[l], dff, seq)
        xf = _matmul(act, wdn_b, l, 0, d, kind="resid", extra=xf, out_dtype=F32, tn_pref=512, name="ffn_down")
    return xf.reshape(bsz, seq, d)
```

```python
import functools
import math

import jax
import jax.numpy as jnp
from jax import lax
from jax.experimental import pallas as pl
from jax.experimental.pallas import tpu as pltpu

NORM_EPS = 1e-6
LN_EPS = 1e-5
NEG_INF = -1e30
SWA_BLOCK = 128
CONV_HALO = 32
FFN_HALO = 8
V7X_VMEM_LIMIT_CAP = 56 * 1024 * 1024
LANES = 128
MXU_K = 256

BF16 = jnp.bfloat16
F32 = jnp.float32


def _cparams(n_axes, vmem_bytes):
    limit = min(V7X_VMEM_LIMIT_CAP, max(32 * 1024 * 1024, int(vmem_bytes * 1.25) + (4 << 20)))
    return pltpu.CompilerParams(dimension_semantics=("arbitrary",) * n_axes, vmem_limit_bytes=limit)


def _tile(n, pref, *also):
    g = n
    for a in also:
        g = math.gcd(g, a)
    best = None
    for t in range(min(pref, g), 0, -1):
        if g % t == 0 and (t % LANES == 0 or t == n):
            best = t
            break
    if best is None:
        raise ValueError(f"no tile for {n} {also}")
    return best


def _row_tile(n, pref):
    for t in range(min(pref, n), 0, -1):
        if n % t == 0 and t % 8 == 0:
            return t
    raise ValueError(f"no row tile for {n}")


def _dot(a, b):
    return jnp.dot(a, b, preferred_element_type=F32)


def _sigmoid(x):
    return 1.0 / (1.0 + jnp.exp(-x))


def _rmsnorm_kernel(x_ref, g_ref, o_ref):
    x = x_ref[...]
    ms = jnp.mean(x * x, axis=-1, keepdims=True)
    o_ref[...] = (x * lax.rsqrt(ms + NORM_EPS) * g_ref[...]).astype(o_ref.dtype)


def _rmsnorm(x, g):
    m, d = x.shape
    tr = _row_tile(m, 256)
    return pl.pallas_call(
        _rmsnorm_kernel,
        out_shape=jax.ShapeDtypeStruct((m, d), BF16),
        grid=(m // tr,),
        in_specs=[pl.BlockSpec((tr, d), lambda i: (i, 0)),
                  pl.BlockSpec((1, d), lambda i: (0, 0))],
        out_specs=pl.BlockSpec((tr, d), lambda i: (i, 0)),
        compiler_params=_cparams(1, 2 * tr * d * 6),
        name="rmsnorm",
    )(x, g.reshape(1, d))


def _mm_plain_kernel(a_ref, b_ref, o_ref):
    o_ref[...] = _dot(a_ref[...], b_ref[...]).astype(o_ref.dtype)


def _mm_headnorm_kernel(a_ref, b_ref, g_ref, o_ref):
    z = _dot(a_ref[...], b_ref[...])
    ms = jnp.mean(z * z, axis=-1, keepdims=True)
    o_ref[...] = (z * lax.rsqrt(ms + NORM_EPS) * g_ref[...]).astype(o_ref.dtype)


def _mm_glu_kernel(a_ref, b1_ref, b2_ref, o_ref):
    a = a_ref[...]
    o_ref[...] = (_dot(a, b1_ref[...]) * _sigmoid(_dot(a, b2_ref[...]))).astype(o_ref.dtype)


def _mm_resid_kernel(a_ref, b_ref, r_ref, o_ref):
    o_ref[...] = r_ref[...] + _dot(a_ref[...], b_ref[...])


def _matmul(a, w, layer, off, n, *, out_dtype, tm_pref=1024, tn_pref=1024, kind="plain", extra=None,
            off2=None, name="matmul"):
    m, k = a.shape
    offs = [off] + ([off2] if off2 is not None else [])
    if kind == "headnorm":
        tn = extra.shape[-1]
        assert n % tn == 0 and off % tn == 0
    else:
        tn = _tile(n, tn_pref, *offs)
    tm = _row_tile(m, tm_pref)
    out_b = jnp.dtype(out_dtype).itemsize
    n_b = 2 if kind == "glu" else 1
    vmem = 2 * (tm * k * 2 + n_b * k * tn * 2 + tm * tn * out_b) + 3 * tm * tn * 4
    a_spec = pl.BlockSpec((tm, k), lambda i, j: (i, 0))

    def b_spec(o):
        return pl.BlockSpec((None, k, tn), lambda i, j: (layer, 0, j + o // tn))

    o_spec = pl.BlockSpec((tm, tn), lambda i, j: (i, j))
    if kind == "plain":
        kern, ins, specs = _mm_plain_kernel, (a, w), [a_spec, b_spec(off)]
    elif kind == "headnorm":
        kern, ins = _mm_headnorm_kernel, (a, w, extra.reshape(1, tn))
        specs = [a_spec, b_spec(off), pl.BlockSpec((1, tn), lambda i, j: (0, 0))]
    elif kind == "glu":
        kern, ins, specs = _mm_glu_kernel, (a, w, w), [a_spec, b_spec(off), b_spec(off2)]
    elif kind == "resid":
        kern, ins = _mm_resid_kernel, (a, w, extra)
        specs = [a_spec, b_spec(off), o_spec]
        vmem += 2 * tm * tn * 4
    else:
        raise ValueError(kind)
    return pl.pallas_call(
        kern,
        out_shape=jax.ShapeDtypeStruct((m, n), out_dtype),
        grid=(m // tm, n // tn),
        in_specs=specs,
        out_specs=o_spec,
        compiler_params=_cparams(2, vmem),
        name=name,
    )(*ins)


def _xattn_kernel(a_ref, b_ref, g_ref, mk_ref, mv_ref, o_ref, *, scale):
    z = _dot(a_ref[...], b_ref[...])
    ms = jnp.mean(z * z, axis=-1, keepdims=True)
    qn = (z * lax.rsqrt(ms + NORM_EPS) * g_ref[...]).astype(BF16)
    s = lax.dot_general(qn, mk_ref[...], (((1,), (1,)), ((), ())), preferred_element_type=F32) * scale
    mx = jnp.max(s, axis=-1, keepdims=True)
    p = jnp.exp(s - mx)
    den = jnp.sum(p, axis=-1, keepdims=True)
    o = _dot(p.astype(BF16), mv_ref[...])
    o_ref[...] = (o / den).astype(o_ref.dtype)


def _xattn(h, w_in, layer, off, xq_gain, mk, mv, seq):
    m, k = h.shape
    bsz, mlen, xq = mk.shape
    hd = xq_gain.shape[-1]
    tm = _row_tile(seq, 1024)
    rows_per_seq = seq // tm
    vmem = 2 * (tm * k * 2 + k * hd * 2 + 2 * mlen * hd * 2 + tm * hd * 2) + 4 * tm * hd * 4 + 3 * tm * mlen * 4
    return pl.pallas_call(
        functools.partial(_xattn_kernel, scale=hd ** -0.5),
        out_shape=jax.ShapeDtypeStruct((m, xq), BF16),
        grid=(m // tm, xq // hd),
        in_specs=[pl.BlockSpec((tm, k), lambda i, j: (i, 0)),
                  pl.BlockSpec((None, k, hd), lambda i, j: (layer, 0, j + off // hd)),
                  pl.BlockSpec((1, hd), lambda i, j: (0, 0)),
                  pl.BlockSpec((None, mlen, hd), lambda i, j: (i // rows_per_seq, 0, j)),
                  pl.BlockSpec((None, mlen, hd), lambda i, j: (i // rows_per_seq, 0, j))],
        out_specs=pl.BlockSpec((tm, hd), lambda i, j: (i, j)),
        compiler_params=_cparams(2, vmem),
        name="xattn",
    )(h, w_in, xq_gain.reshape(1, hd), mk, mv)


def _pair_rms(x, lo, gain2):
    hd = LANES // 2
    x2 = x * x
    s_lo = jnp.sum(jnp.where(lo, x2, 0.0), axis=-1, keepdims=True)
    s_hi = jnp.sum(jnp.where(lo, 0.0, x2), axis=-1, keepdims=True)
    r = jnp.where(lo, lax.rsqrt(s_lo * (1.0 / hd) + NORM_EPS), lax.rsqrt(s_hi * (1.0 / hd) + NORM_EPS))
    return x * r * gain2


def _both_halves(col, lo, half):
    keep = lo if half == 0 else jnp.logical_not(lo)
    base = jnp.where(keep, col, 0.0)
    other = pltpu.roll(base, LANES // 2, axis=1)
    return jnp.concatenate([base, other] if half == 0 else [other, base], axis=0)


def _swa_kernel(sink_ref, q_ref, kc_ref, kp_ref, vc_ref, vp_ref, qg_ref, kg_ref, o_ref, *, hq, hkv, hd):
    n = pl.program_id(1)
    blk = SWA_BLOCK
    grp = hq // hkv
    tiles = grp // 2
    rows = tiles * blk
    lane = lax.broadcasted_iota(jnp.int32, (1, LANES), 1)
    lo = lane < hd
    qi = lax.broadcasted_iota(jnp.int32, (blk, 2 * blk), 0)
    kj = lax.broadcasted_iota(jnp.int32, (blk, 2 * blk), 1)
    delta = qi + blk - kj
    valid = (delta >= 0) & (delta < blk) & ((kj >= blk) | (n > 0))
    key0 = lax.broadcasted_iota(jnp.int32, (4 * blk, 1), 0) & (2 * blk - 1) == 0
    qg2 = qg_ref[...] * (hd ** -0.5)
    kg2 = kg_ref[...]
    heads = range(hkv)
    kpads, vpads, qns = [], [], []
    for c2 in range(hkv // 2):
        csl = slice(c2 * LANES, (c2 + 1) * LANES)
        kcol = _pair_rms(jnp.concatenate([kp_ref[:, csl], kc_ref[:, csl]], axis=0), lo, kg2)
        vcol = jnp.concatenate([vp_ref[:, csl], vc_ref[:, csl]], axis=0)
        for half in range(2):
            kpads.append(_both_halves(kcol, lo, half).astype(BF16))
            vpads.append(jnp.where(key0, 0.0, _both_halves(vcol, lo, half)).astype(BF16))
    for h in heads:
        q4 = jnp.concatenate([q_ref[:, (h * tiles + t) * LANES:(h * tiles + t + 1) * LANES]
                              for t in range(tiles)], axis=0)
        qns.append(_pair_rms(q4, lo, qg2).astype(BF16))
    scores = [lax.dot_general(qns[h], kpads[h], (((1,), (1,)), ((), ())), preferred_element_type=F32)
              for h in heads]
    probs, scales = [], []
    for h in heads:
        ps, rds = [], []
        for e in range(2):
            cols = []
            for t in range(tiles):
                st = jnp.where(valid, scores[h][t * blk:(t + 1) * blk, e * 2 * blk:(e + 1) * 2 * blk], NEG_INF)
                first = jnp.where(lane == 0, sink_ref[h * grp + 2 * t + e], st[:, :LANES])
                cols.append(jnp.concatenate([first, st[:, LANES:]], axis=1))
            se = jnp.concatenate(cols, axis=0)
            mx = jnp.max(se, axis=-1, keepdims=True)
            p = jnp.exp(se - mx)
            rds.append(1.0 / jnp.sum(p, axis=-1, keepdims=True))
            ps.append(p.astype(BF16))
        probs.append(jnp.concatenate(ps, axis=1))
        scales.append(jnp.where(lo, rds[0], rds[1]))
    for h in heads:
        o4 = _dot(probs[h], vpads[h]) * scales[h]
        for t in range(tiles):
            o_ref[:, (h * tiles + t) * LANES:(h * tiles + t + 1) * LANES] = (
                o4[t * blk:(t + 1) * blk, :].astype(o_ref.dtype))


def _swa(qkv, sinks, q_gain, k_gain, *, hq, hkv, hd):
    bsz, seq, _ = qkv.shape
    qw, kvw = hq * hd, hkv * hd
    blk = SWA_BLOCK
    nb = seq // blk
    kb = qw // kvw
    assert qw % kvw == 0 and seq % blk == 0
    assert 2 * hd == LANES and hkv % 2 == 0 and (hq // hkv) % 2 == 0
    q_gain = jnp.concatenate([q_gain, q_gain])
    k_gain = jnp.concatenate([k_gain, k_gain])

    def spec(width, col, prev):
        if prev:
            return pl.BlockSpec((None, blk, width), lambda b, n: (b, jnp.maximum(n - 1, 0), col))
        return pl.BlockSpec((None, blk, width), lambda b, n: (b, n, col))

    return pl.pallas_call(
        functools.partial(_swa_kernel, hq=hq, hkv=hkv, hd=hd),
        out_shape=jax.ShapeDtypeStruct((bsz, seq, qw), BF16),
        grid=(bsz, nb),
        in_specs=[pl.BlockSpec(memory_space=pltpu.SMEM),
                  spec(qw, 0, False),
                  spec(kvw, kb, False), spec(kvw, kb, True),
                  spec(kvw, kb + 1, False), spec(kvw, kb + 1, True),
                  pl.BlockSpec((1, LANES), lambda b, n: (0, 0)),
                  pl.BlockSpec((1, LANES), lambda b, n: (0, 0))],
        out_specs=pl.BlockSpec((None, blk, qw), lambda b, n: (b, n, 0)),
        compiler_params=_cparams(2, 2 * blk * (qw * 6 + 4 * kvw * 4)),
        name="swa",
    )(sinks, qkv, qkv, qkv, qkv, qkv, q_gain.reshape(1, LANES), k_gain.reshape(1, LANES))


def _dwconv_ln_kernel(cur_ref, halo_ref, w_ref, b_ref, g_ref, beta_ref, o_ref, ext_ref, y_ref, sh_ref, *, ksize):
    i = pl.program_id(1)
    ts, c = cur_ref.shape
    keep = (i > 0).astype(F32)
    ext_ref[0:CONV_HALO, :] = halo_ref[...] * keep
    ext_ref[CONV_HALO:, :] = cur_ref[...]
    base = CONV_HALO - (ksize - 1)

    rb = min(ts, 128)

    def chunk(cc, carry):
        c0 = pl.multiple_of(cc * LANES, LANES)
        lanes = pl.ds(c0, LANES)
        nrows = ts + CONV_HALO - 8
        for s in range(1, 8):
            sh_ref[s - 1, 0:nrows, :] = ext_ref[pl.ds(s, nrows), lanes]
        for r0 in range(0, ts, rb):
            acc = jnp.broadcast_to(b_ref[:, lanes], (rb, LANES))
            for k in range(ksize):
                s, a8 = (base + k) % 8, 8 * ((base + k) // 8)
                if s == 0:
                    win = ext_ref[pl.ds(r0 + a8, rb), lanes]
                else:
                    win = sh_ref[s - 1, pl.ds(r0 + a8, rb), :]
                acc = acc + w_ref[k:k + 1, lanes] * win
            y_ref[pl.ds(r0, rb), lanes] = acc
        return carry

    lax.fori_loop(0, c // LANES, chunk, 0)
    y = y_ref[...]
    mu = jnp.mean(y, axis=-1, keepdims=True)
    yc = y - mu
    var = jnp.mean(yc * yc, axis=-1, keepdims=True)
    z = yc * lax.rsqrt(var + LN_EPS) * g_ref[...] + beta_ref[...]
    o_ref[...] = (z * _sigmoid(z)).astype(o_ref.dtype)


def _dwconv_ln(c0, w, b, g, beta):
    bsz, seq, c = c0.shape
    ksize = w.shape[0]
    assert ksize - 1 <= CONV_HALO and c % LANES == 0
    ts = _row_tile(seq, 256)
    assert ts % CONV_HALO == 0
    r = ts // CONV_HALO
    return pl.pallas_call(
        functools.partial(_dwconv_ln_kernel, ksize=ksize),
        out_shape=jax.ShapeDtypeStruct((bsz, seq, c), BF16),
        grid=(bsz, seq // ts),
        in_specs=[pl.BlockSpec((None, ts, c), lambda bb, i: (bb, i, 0)),
                  pl.BlockSpec((None, CONV_HALO, c), lambda bb, i: (bb, jnp.maximum(i * r - 1, 0), 0)),
                  pl.BlockSpec((ksize, c), lambda bb, i: (0, 0)),
                  pl.BlockSpec((1, c), lambda bb, i: (0, 0)),
                  pl.BlockSpec((1, c), lambda bb, i: (0, 0)),
                  pl.BlockSpec((1, c), lambda bb, i: (0, 0))],
        out_specs=pl.BlockSpec((None, ts, c), lambda bb, i: (bb, i, 0)),
        scratch_shapes=[pltpu.VMEM((ts + CONV_HALO, c), F32), pltpu.VMEM((ts, c), F32),
                        pltpu.VMEM((7, ts + CONV_HALO - 8, LANES), F32)],
        compiler_params=_cparams(2, ts * c * 4 * 8),
        name="dwconv_ln",
    )(c0, c0, w, b.reshape(1, c), g.reshape(1, c), beta.reshape(1, c))


def _zero_fill(refs):
    rows = refs[0].shape[0]

    def body(i, carry):
        r = pl.multiple_of(i * 8, 8)
        for ref in refs:
            ref[pl.ds(r, 8), :] = jnp.zeros((8, ref.shape[1]), F32)
        return carry

    lax.fori_loop(0, rows // 8, body, 0)


def _pipelined_kernel(ti_ref, tj_ref, *refs, n_a, dot_a, e_tile, epilogue, row_chunk):
    n_d, n_e = len(dot_a), len(e_tile)
    a_refs, b_refs = refs[:n_a], refs[n_a:n_a + n_d]
    e_refs = refs[n_a + n_d:n_a + n_d + n_e]
    o_ref = refs[n_a + n_d + n_e]
    z_even, z_odd = refs[-2 * n_d:-n_d], refs[-n_d:]
    t = pl.program_id(0)
    tm = o_ref.shape[0]

    @pl.when(t == 0)
    def _():
        _zero_fill(z_odd)

    def step(zr, zw):
        def epilogue_rows(r0, r1):
            for r in range(r0, r1, row_chunk):
                rs = slice(r, r + row_chunk)
                es = [e[rs, :] if tile else e[...] for e, tile in zip(e_refs, e_tile)]
                o_ref[rs, :] = epilogue([z[rs, :] for z in zr], es).astype(o_ref.dtype)

        chunks = [(d, kk) for d in range(n_d)
                  for kk in range(a_refs[dot_a[d]].shape[1] // _tile(a_refs[dot_a[d]].shape[1], MXU_K))]
        bounds = [(tm * s // len(chunks)) // row_chunk * row_chunk for s in range(len(chunks) + 1)]
        acc = None
        for s, (d, kk) in enumerate(chunks):
            a_ref, b_ref = a_refs[dot_a[d]], b_refs[d]
            kc = _tile(a_ref.shape[1], MXU_K)
            part = _dot(a_ref[:, kk * kc:(kk + 1) * kc], b_ref[kk * kc:(kk + 1) * kc, :])
            acc = part if kk == 0 else acc + part
            epilogue_rows(bounds[s], bounds[s + 1])
            if (kk + 1) * kc == a_ref.shape[1]:
                zw[d][...] = acc

    parity = lax.rem(t, 2)

    @pl.when(parity == 0)
    def _():
        step(z_odd, z_even)

    @pl.when(parity == 1)
    def _():
        step(z_even, z_odd)


def _pipelined_matmul(name, m, n, tm, tn, a_ops, b_ops, e_ops, out_dtype, epilogue, row_chunk=32):
    ni, nj = m // tm, n // tn
    steps = ni * nj
    order = [min(max(s - 1, 0), steps - 1) for s in range(steps + 2)]
    ti = jnp.asarray([s // nj for s in order], jnp.int32)
    tj = jnp.asarray([s % nj for s in order], jnp.int32)
    specs, args = [], []
    vmem = 2 * tm * tn * jnp.dtype(out_dtype).itemsize + 2 * len(b_ops) * tm * tn * 4
    for a in a_ops:
        specs.append(pl.BlockSpec((tm, a.shape[1]), lambda t, ti, tj: (ti[t + 1], 0)))
        args.append(a)
        vmem += 2 * tm * a.shape[1] * 2
    for w, layer, off, _ in b_ops:
        assert off % tn == 0
        specs.append(pl.BlockSpec((None, w.shape[1], tn),
                                  lambda t, ti, tj, layer=layer, o=off // tn: (layer, 0, tj[t + 1] + o)))
        args.append(w)
        vmem += 2 * w.shape[1] * tn * 2
    e_tile = []
    for e, off in e_ops:
        assert off % tn == 0
        tile = e.shape[0] != 1
        e_tile.append(tile)
        if tile:
            specs.append(pl.BlockSpec((tm, tn), lambda t, ti, tj, o=off // tn: (ti[t], tj[t] + o)))
        else:
            specs.append(pl.BlockSpec((1, tn), lambda t, ti, tj, o=off // tn: (0, tj[t] + o)))
        args.append(e)
        vmem += 2 * (tm if tile else 8) * tn * e.dtype.itemsize
    zbuf = pltpu.VMEM((tm, tn), F32)
    return pl.pallas_call(
        functools.partial(_pipelined_kernel, n_a=len(a_ops), dot_a=tuple(b[3] for b in b_ops),
                          e_tile=tuple(e_tile), epilogue=epilogue, row_chunk=min(tm, row_chunk)),
        out_shape=jax.ShapeDtypeStruct((m, n), out_dtype),
        grid_spec=pltpu.PrefetchScalarGridSpec(
            num_scalar_prefetch=2,
            grid=(steps + 1,),
            in_specs=specs,
            out_specs=pl.BlockSpec((tm, tn), lambda t, ti, tj: (ti[t], tj[t])),
            scratch_shapes=[zbuf] * (2 * len(b_ops))),
        compiler_params=_cparams(1, vmem),
        name=name,
    )(ti, tj, *args)


def _merge_epilogue(zs, es):
    y = es[0].astype(F32) * zs[0]
    y = y + es[1].astype(F32) * zs[1]
    return y + es[2].astype(F32) * zs[2]


def _merge(a, c, xo, wa, wc, wx, gates, layer, d):
    m = a.shape[0]
    tm, tn = _row_tile(m, 512), _tile(d, 512)
    return _pipelined_matmul("merge", m, d, tm, tn, [a, c, xo],
                             [(wa, layer, 0, 0), (wc, layer, 0, 1), (wx, layer, 0, 2)],
                             [(gates, 0), (gates, d), (gates, 2 * d)], BF16, _merge_epilogue)


def _gate_epilogue(zs, es):
    return _sigmoid(zs[0] + es[0])


def _ffn_up_kernel(ti_ref, tj_ref, a_ref, b1_ref, b2_ref, w1_ref, w2_ref, c1_ref, c2_ref, o_ref,
                   za1, za2, zb1, zb2, carry1, carry2, *, ksize, tiles_per_seq, row_chunk):
    t = pl.program_id(0)
    tm = a_ref.shape[0]
    halo = FFN_HALO

    @pl.when(t == 0)
    def _():
        def zero_rows(i, carry):
            r = pl.multiple_of(i * halo, halo)
            zb1[pl.ds(r, halo), :] = jnp.zeros((halo, zb1.shape[1]), F32)
            zb2[pl.ds(r, halo), :] = jnp.zeros((halo, zb2.shape[1]), F32)
            return carry

        lax.fori_loop(0, zb1.shape[0] // halo, zero_rows, 0)

    def step(zr1, zr2, zw1, zw2):
        i_prev, j_prev = ti_ref[t], tj_ref[t]
        seq_start = lax.rem(i_prev, tiles_per_seq) == 0

        @pl.when(seq_start)
        def _():
            zr1[0:halo, :] = jnp.zeros((halo, zr1.shape[1]), F32)
            zr2[0:halo, :] = jnp.zeros((halo, zr2.shape[1]), F32)
            carry1[j_prev] = zr1[tm:tm + halo, :]
            carry2[j_prev] = zr2[tm:tm + halo, :]

        @pl.when(jnp.logical_not(seq_start))
        def _():
            zr1[0:halo, :] = carry1[j_prev]
            zr2[0:halo, :] = carry2[j_prev]
            carry1[j_prev] = zr1[tm:tm + halo, :]
            carry2[j_prev] = zr2[tm:tm + halo, :]

        def conv(zr, w_ref, c_ref, r):
            acc = c_ref[...] + w_ref[ksize - 1:ksize, :] * zr[halo + r:halo + r + row_chunk, :]
            for k in range(ksize - 1):
                shift = ksize - 1 - k
                acc = acc + w_ref[k:k + 1, :] * zr[pl.ds(halo + r - shift, row_chunk), :]
            return acc

        def epilogue_rows(r0, r1):
            for r in range(r0, r1, row_chunk):
                u1 = conv(zr1, w1_ref, c1_ref, r)
                u2 = conv(zr2, w2_ref, c2_ref, r)
                o_ref[r:r + row_chunk, :] = (u1 * _sigmoid(u1) * u2).astype(o_ref.dtype)

        kdim = a_ref.shape[1]
        kc = _tile(kdim, MXU_K)
        nk = kdim // kc
        bounds = [(tm * s // (2 * nk)) // row_chunk * row_chunk for s in range(2 * nk + 1)]
        slot = 0
        for zw, b_ref in ((zw1, b1_ref), (zw2, b2_ref)):
            acc = None
            for kk in range(nk):
                part = _dot(a_ref[:, kk * kc:(kk + 1) * kc], b_ref[kk * kc:(kk + 1) * kc, :])
                acc = part if acc is None else acc + part
                epilogue_rows(bounds[slot], bounds[slot + 1])
                slot += 1
            zw[halo:, :] = acc

    parity = lax.rem(t, 2)

    @pl.when(parity == 0)
    def _():
        step(zb1, zb2, za1, za2)

    @pl.when(parity == 1)
    def _():
        step(za1, za2, zb1, zb2)


def _ffn_up(h, w_up, layer, conv_w, conv_b, dff, seq):
    m, k = h.shape
    ksize = conv_w.shape[0]
    assert ksize - 1 <= FFN_HALO
    tm = _row_tile(seq, 1024)
    tn = _tile(dff, 512)
    ni, nf = m // tm, dff // tn
    steps = ni * nf
    order = [min(max(s - 1, 0), steps - 1) for s in range(steps + 2)]
    ti = jnp.asarray([s // nf for s in order], jnp.int32)
    tj = jnp.asarray([s % nf for s in order], jnp.int32)
    vmem = 2 * (tm * k * 2 + 2 * k * tn * 2 + tm * tn * 2) + 4 * (tm + FFN_HALO) * tn * 4 + 2 * nf * FFN_HALO * tn * 4

    def wspec(rows, which):
        return pl.BlockSpec((rows, tn), lambda t, ti, tj: (0, tj[t] + which * nf))

    def bspec(which):
        return pl.BlockSpec((None, k, tn), lambda t, ti, tj: (layer, 0, tj[t + 1] + which * nf))

    zbuf = pltpu.VMEM((tm + FFN_HALO, tn), F32)
    carry = pltpu.VMEM((nf, FFN_HALO, tn), F32)
    two_f = 2 * dff
    return pl.pallas_call(
        functools.partial(_ffn_up_kernel, ksize=ksize, tiles_per_seq=seq // tm, row_chunk=min(tm, 32)),
        out_shape=jax.ShapeDtypeStruct((m, dff), BF16),
        grid_spec=pltpu.PrefetchScalarGridSpec(
            num_scalar_prefetch=2,
            grid=(steps + 1,),
            in_specs=[pl.BlockSpec((tm, k), lambda t, ti, tj: (ti[t + 1], 0)),
                      bspec(0), bspec(1), wspec(ksize, 0), wspec(ksize, 1), wspec(1, 0), wspec(1, 1)],
            out_specs=pl.BlockSpec((tm, tn), lambda t, ti, tj: (ti[t], tj[t])),
            scratch_shapes=[zbuf, zbuf, zbuf, zbuf, carry, carry]),
        compiler_params=_cparams(1, vmem),
        name="ffn_up",
    )(ti, tj, h, w_up, w_up, conv_w, conv_w, conv_b.reshape(1, two_f), conv_b.reshape(1, two_f))


def kernel(x, mem, attn_norm, w_in, swa_q_norm, swa_k_norm, swa_sinks, conv_dw_w, conv_dw_b, conv_ln_g, conv_ln_b,
           mem_norm, w_mem_kv, x_q_norm, x_k_norm, gate_b, w_proj_a, w_proj_c, w_proj_x, w_o, ffn_norm, w_up,
           ffn_dw_w, ffn_dw_b, w_down):
    bsz, seq, d = x.shape
    mlen = mem.shape[1]
    depth = w_in.shape[0]
    hq, hd = swa_sinks.shape[1], swa_q_norm.shape[1]
    swa_q = w_proj_a.shape[1]
    conv_w = conv_dw_w.shape[2]
    x_q = w_mem_kv.shape[2] // 2
    xhd = x_q_norm.shape[1]
    dff = w_down.shape[1]
    swa_kv = (w_in.shape[2] - swa_q - 2 * conv_w - x_q - 3 * d) // 2
    hkv = swa_kv // hd
    assert swa_q == hq * hd and hkv * hd == swa_kv
    off_glu = swa_q + 2 * swa_kv
    off_xq = off_glu + 2 * conv_w
    off_gate = off_xq + x_q
    m = bsz * seq

    w_in_b, w_mem_b = w_in.astype(BF16), w_mem_kv.astype(BF16)
    wa_b, wc_b, wx_b = w_proj_a.astype(BF16), w_proj_c.astype(BF16), w_proj_x.astype(BF16)
    wo_b, wup_b, wdn_b = w_o.astype(BF16), w_up.astype(BF16), w_down.astype(BF16)

    xf = x.reshape(m, d)
    memf = mem.reshape(bsz * mlen, d)
    for l in range(depth):
        h = _rmsnorm(xf, attn_norm[l])
        qkv = _matmul(h, w_in_b, l, 0, off_glu, out_dtype=F32, tn_pref=512, name="inproj_qkv")
        c0 = _matmul(h, w_in_b, l, off_glu, conv_w, off2=off_glu + conv_w, kind="glu", out_dtype=F32,
                     tn_pref=512, name="inproj_glu")
        gates = _pipelined_matmul("inproj_gates", m, 3 * d, _row_tile(m, 1024), _tile(3 * d, 512, off_gate), [h],
                                  [(w_in_b, l, off_gate, 0)], [(gate_b[l].reshape(1, 3 * d), 0)], BF16,
                                  _gate_epilogue)
        mn = _rmsnorm(memf, mem_norm[l])
        mk = _matmul(mn, w_mem_b, l, 0, x_q, kind="headnorm", extra=x_k_norm[l], out_dtype=BF16, name="mem_k")
        mv = _matmul(mn, w_mem_b, l, x_q, x_q, out_dtype=BF16, tn_pref=512, name="mem_v")
        xo = _xattn(h, w_in_b, l, off_xq, x_q_norm[l], mk.reshape(bsz, mlen, x_q), mv.reshape(bsz, mlen, x_q), seq)
        a = _swa(qkv.reshape(bsz, seq, off_glu), swa_sinks[l], swa_q_norm[l], swa_k_norm[l], hq=hq, hkv=hkv, hd=hd)
        c = _dwconv_ln(c0.reshape(bsz, seq, conv_w), conv_dw_w[l], conv_dw_b[l], conv_ln_g[l], conv_ln_b[l])
        y = _merge(a.reshape(m, swa_q), c.reshape(m, conv_w), xo, wa_b, wc_b, wx_b, gates, l, d)
        xf = _matmul(y, wo_b, l, 0, d, kind="resid", extra=xf, out_dtype=F32, name="out_proj")
        h2 = _rmsnorm(xf, ffn_norm[l])
        act = _ffn_up(h2, wup_b, l, ffn_dw_w[l], ffn_dw_b[l], dff, seq)
        xf = _matmul(act, wdn_b, l, 0, d, kind="resid", extra=xf, out_dtype=F32, tn_pref=512, name="ffn_down")
    return xf.reshape(bsz, seq, d)
```

```python
import functools
import math

import jax
import jax.numpy as jnp
from jax import lax
from jax.experimental import pallas as pl
from jax.experimental.pallas import tpu as pltpu

NORM_EPS = 1e-6
LN_EPS = 1e-5
NEG_INF = -1e30
SWA_BLOCK = 128
CONV_HALO = 32
FFN_HALO = 8
V7X_VMEM_LIMIT_CAP = 56 * 1024 * 1024
LANES = 128
MXU_K = 256

BF16 = jnp.bfloat16
F32 = jnp.float32


def _cparams(n_axes, vmem_bytes):
    limit = min(V7X_VMEM_LIMIT_CAP, max(32 * 1024 * 1024, int(vmem_bytes * 1.25) + (4 << 20)))
    return pltpu.CompilerParams(dimension_semantics=("arbitrary",) * n_axes, vmem_limit_bytes=limit)


def _tile(n, pref, *also):
    g = n
    for a in also:
        g = math.gcd(g, a)
    best = None
    for t in range(min(pref, g), 0, -1):
        if g % t == 0 and (t % LANES == 0 or t == n):
            best = t
            break
    if best is None:
        raise ValueError(f"no tile for {n} {also}")
    return best


def _row_tile(n, pref):
    for t in range(min(pref, n), 0, -1):
        if n % t == 0 and t % 8 == 0:
            return t
    raise ValueError(f"no row tile for {n}")


def _dot(a, b):
    return jnp.dot(a, b, preferred_element_type=F32)


def _sigmoid(x):
    return 1.0 / (1.0 + jnp.exp(-x))


def _rmsnorm_kernel(x_ref, g_ref, o_ref):
    x = x_ref[...]
    ms = jnp.mean(x * x, axis=-1, keepdims=True)
    o_ref[...] = (x * lax.rsqrt(ms + NORM_EPS) * g_ref[...]).astype(o_ref.dtype)


def _rmsnorm(x, g):
    m, d = x.shape
    tr = _row_tile(m, 256)
    return pl.pallas_call(
        _rmsnorm_kernel,
        out_shape=jax.ShapeDtypeStruct((m, d), BF16),
        grid=(m // tr,),
        in_specs=[pl.BlockSpec((tr, d), lambda i: (i, 0)),
                  pl.BlockSpec((1, d), lambda i: (0, 0))],
        out_specs=pl.BlockSpec((tr, d), lambda i: (i, 0)),
        compiler_params=_cparams(1, 2 * tr * d * 6),
        name="rmsnorm",
    )(x, g.reshape(1, d))


def _mm_plain_kernel(a_ref, b_ref, o_ref):
    o_ref[...] = _dot(a_ref[...], b_ref[...]).astype(o_ref.dtype)


def _mm_headnorm_kernel(a_ref, b_ref, g_ref, o_ref):
    z = _dot(a_ref[...], b_ref[...])
    ms = jnp.mean(z * z, axis=-1, keepdims=True)
    o_ref[...] = (z * lax.rsqrt(ms + NORM_EPS) * g_ref[...]).astype(o_ref.dtype)


def _mm_glu_kernel(a_ref, b1_ref, b2_ref, o_ref):
    a = a_ref[...]
    o_ref[...] = (_dot(a, b1_ref[...]) * _sigmoid(_dot(a, b2_ref[...]))).astype(o_ref.dtype)


def _mm_resid_kernel(a_ref, b_ref, r_ref, o_ref):
    o_ref[...] = r_ref[...] + _dot(a_ref[...], b_ref[...])


def _matmul(a, w, layer, off, n, *, out_dtype, tm_pref=1024, tn_pref=1024, kind="plain", extra=None,
            off2=None, name="matmul"):
    m, k = a.shape
    offs = [off] + ([off2] if off2 is not None else [])
    if kind == "headnorm":
        tn = extra.shape[-1]
        assert n % tn == 0 and off % tn == 0
    else:
        tn = _tile(n, tn_pref, *offs)
    tm = _row_tile(m, tm_pref)
    out_b = jnp.dtype(out_dtype).itemsize
    n_b = 2 if kind == "glu" else 1
    vmem = 2 * (tm * k * 2 + n_b * k * tn * 2 + tm * tn * out_b) + 3 * tm * tn * 4
    a_spec = pl.BlockSpec((tm, k), lambda i, j: (i, 0))

    def b_spec(o):
        return pl.BlockSpec((None, k, tn), lambda i, j: (layer, 0, j + o // tn))

    o_spec = pl.BlockSpec((tm, tn), lambda i, j: (i, j))
    if kind == "plain":
        kern, ins, specs = _mm_plain_kernel, (a, w), [a_spec, b_spec(off)]
    elif kind == "headnorm":
        kern, ins = _mm_headnorm_kernel, (a, w, extra.reshape(1, tn))
        specs = [a_spec, b_spec(off), pl.BlockSpec((1, tn), lambda i, j: (0, 0))]
    elif kind == "glu":
        kern, ins, specs = _mm_glu_kernel, (a, w, w), [a_spec, b_spec(off), b_spec(off2)]
    elif kind == "resid":
        kern, ins = _mm_resid_kernel, (a, w, extra)
        specs = [a_spec, b_spec(off), o_spec]
        vmem += 2 * tm * tn * 4
    else:
        raise ValueError(kind)
    return pl.pallas_call(
        kern,
        out_shape=jax.ShapeDtypeStruct((m, n), out_dtype),
        grid=(m // tm, n // tn),
        in_specs=specs,
        out_specs=o_spec,
        compiler_params=_cparams(2, vmem),
        name=name,
    )(*ins)


def _xattn_kernel(a_ref, b_ref, g_ref, mk_ref, mv_ref, o_ref, *, scale):
    z = _dot(a_ref[...], b_ref[...])
    ms = jnp.mean(z * z, axis=-1, keepdims=True)
    qn = (z * lax.rsqrt(ms + NORM_EPS) * g_ref[...]).astype(BF16)
    s = lax.dot_general(qn, mk_ref[...], (((1,), (1,)), ((), ())), preferred_element_type=F32) * scale
    mx = jnp.max(s, axis=-1, keepdims=True)
    p = jnp.exp(s - mx)
    den = jnp.sum(p, axis=-1, keepdims=True)
    o = _dot(p.astype(BF16), mv_ref[...])
    o_ref[...] = (o / den).astype(o_ref.dtype)


def _xattn(h, w_in, layer, off, xq_gain, mk, mv, seq):
    m, k = h.shape
    bsz, mlen, xq = mk.shape
    hd = xq_gain.shape[-1]
    tm = _row_tile(seq, 1024)
    rows_per_seq = seq // tm
    vmem = 2 * (tm * k * 2 + k * hd * 2 + 2 * mlen * hd * 2 + tm * hd * 2) + 4 * tm * hd * 4 + 3 * tm * mlen * 4
    return pl.pallas_call(
        functools.partial(_xattn_kernel, scale=hd ** -0.5),
        out_shape=jax.ShapeDtypeStruct((m, xq), BF16),
        grid=(m // tm, xq // hd),
        in_specs=[pl.BlockSpec((tm, k), lambda i, j: (i, 0)),
                  pl.BlockSpec((None, k, hd), lambda i, j: (layer, 0, j + off // hd)),
                  pl.BlockSpec((1, hd), lambda i, j: (0, 0)),
                  pl.BlockSpec((None, mlen, hd), lambda i, j: (i // rows_per_seq, 0, j)),
                  pl.BlockSpec((None, mlen, hd), lambda i, j: (i // rows_per_seq, 0, j))],
        out_specs=pl.BlockSpec((tm, hd), lambda i, j: (i, j)),
        compiler_params=_cparams(2, vmem),
        name="xattn",
    )(h, w_in, xq_gain.reshape(1, hd), mk, mv)


def _pair_rms(x, lo, gain2):
    hd = LANES // 2
    x2 = x * x
    s_lo = jnp.sum(jnp.where(lo, x2, 0.0), axis=-1, keepdims=True)
    s_hi = jnp.sum(jnp.where(lo, 0.0, x2), axis=-1, keepdims=True)
    r = jnp.where(lo, lax.rsqrt(s_lo * (1.0 / hd) + NORM_EPS), lax.rsqrt(s_hi * (1.0 / hd) + NORM_EPS))
    return x * r * gain2


def _both_halves(col, lo, half):
    keep = lo if half == 0 else jnp.logical_not(lo)
    base = jnp.where(keep, col, 0.0)
    other = pltpu.roll(base, LANES // 2, axis=1)
    return jnp.concatenate([base, other] if half == 0 else [other, base], axis=0)


def _swa_kernel(sink_ref, q_ref, kc_ref, kp_ref, vc_ref, vp_ref, qg_ref, kg_ref, o_ref, *, hq, hkv, hd):
    n = pl.program_id(1)
    blk = SWA_BLOCK
    grp = hq // hkv
    tiles = grp // 2
    rows = tiles * blk
    lane = lax.broadcasted_iota(jnp.int32, (1, LANES), 1)
    lo = lane < hd
    qi = lax.broadcasted_iota(jnp.int32, (blk, 2 * blk), 0)
    kj = lax.broadcasted_iota(jnp.int32, (blk, 2 * blk), 1)
    delta = qi + blk - kj
    valid = (delta >= 0) & (delta < blk) & ((kj >= blk) | (n > 0))
    key0 = lax.broadcasted_iota(jnp.int32, (4 * blk, 1), 0) & (2 * blk - 1) == 0
    qg2 = qg_ref[...] * (hd ** -0.5)
    kg2 = kg_ref[...]
    heads = range(hkv)
    kpads, vpads, qns = [], [], []
    for c2 in range(hkv // 2):
        csl = slice(c2 * LANES, (c2 + 1) * LANES)
        kcol = _pair_rms(jnp.concatenate([kp_ref[:, csl], kc_ref[:, csl]], axis=0), lo, kg2)
        vcol = jnp.concatenate([vp_ref[:, csl], vc_ref[:, csl]], axis=0)
        for half in range(2):
            kpads.append(_both_halves(kcol, lo, half).astype(BF16))
            vpads.append(jnp.where(key0, 0.0, _both_halves(vcol, lo, half)).astype(BF16))
    for h in heads:
        q4 = jnp.concatenate([q_ref[:, (h * tiles + t) * LANES:(h * tiles + t + 1) * LANES]
                              for t in range(tiles)], axis=0)
        qns.append(_pair_rms(q4, lo, qg2).astype(BF16))
    scores = [lax.dot_general(qns[h], kpads[h], (((1,), (1,)), ((), ())), preferred_element_type=F32)
              for h in heads]
    probs, scales = [], []
    for h in heads:
        ps, rds = [], []
        for e in range(2):
            cols = []
            for t in range(tiles):
                st = jnp.where(valid, scores[h][t * blk:(t + 1) * blk, e * 2 * blk:(e + 1) * 2 * blk], NEG_INF)
                first = jnp.where(lane == 0, sink_ref[h * grp + 2 * t + e], st[:, :LANES])
                cols.append(jnp.concatenate([first, st[:, LANES:]], axis=1))
            se = jnp.concatenate(cols, axis=0)
            mx = jnp.max(se, axis=-1, keepdims=True)
            p = jnp.exp(se - mx)
            rds.append(1.0 / jnp.sum(p, axis=-1, keepdims=True))
            ps.append(p.astype(BF16))
        probs.append(jnp.concatenate(ps, axis=1))
        scales.append(jnp.where(lo, rds[0], rds[1]))
    for h in heads:
        o4 = _dot(probs[h], vpads[h]) * scales[h]
        for t in range(tiles):
            o_ref[:, (h * tiles + t) * LANES:(h * tiles + t + 1) * LANES] = (
                o4[t * blk:(t + 1) * blk, :].astype(o_ref.dtype))


def _swa(qkv, sinks, q_gain, k_gain, *, hq, hkv, hd):
    bsz, seq, _ = qkv.shape
    qw, kvw = hq * hd, hkv * hd
    blk = SWA_BLOCK
    nb = seq // blk
    kb = qw // kvw
    assert qw % kvw == 0 and seq % blk == 0
    assert 2 * hd == LANES and hkv % 2 == 0 and (hq // hkv) % 2 == 0
    q_gain = jnp.concatenate([q_gain, q_gain])
    k_gain = jnp.concatenate([k_gain, k_gain])

    def spec(width, col, prev):
        if prev:
            return pl.BlockSpec((None, blk, width), lambda b, n: (b, jnp.maximum(n - 1, 0), col))
        return pl.BlockSpec((None, blk, width), lambda b, n: (b, n, col))

    return pl.pallas_call(
        functools.partial(_swa_kernel, hq=hq, hkv=hkv, hd=hd),
        out_shape=jax.ShapeDtypeStruct((bsz, seq, qw), BF16),
        grid=(bsz, nb),
        in_specs=[pl.BlockSpec(memory_space=pltpu.SMEM),
                  spec(qw, 0, False),
                  spec(kvw, kb, False), spec(kvw, kb, True),
                  spec(kvw, kb + 1, False), spec(kvw, kb + 1, True),
                  pl.BlockSpec((1, LANES), lambda b, n: (0, 0)),
                  pl.BlockSpec((1, LANES), lambda b, n: (0, 0))],
        out_specs=pl.BlockSpec((None, blk, qw), lambda b, n: (b, n, 0)),
        compiler_params=_cparams(2, 2 * blk * (qw * 6 + 4 * kvw * 4)),
        name="swa",
    )(sinks, qkv, qkv, qkv, qkv, qkv, q_gain.reshape(1, LANES), k_gain.reshape(1, LANES))


def _dwconv_ln_kernel(cur_ref, halo_ref, w_ref, b_ref, g_ref, beta_ref, o_ref, ext_ref, y_ref, sh_ref, *, ksize):
    i = pl.program_id(1)
    ts, c = cur_ref.shape
    keep = (i > 0).astype(F32)
    ext_ref[0:CONV_HALO, :] = halo_ref[...] * keep
    ext_ref[CONV_HALO:, :] = cur_ref[...]
    base = CONV_HALO - (ksize - 1)

    rb = min(ts, 128)

    def chunk(cc, carry):
        c0 = pl.multiple_of(cc * LANES, LANES)
        lanes = pl.ds(c0, LANES)
        nrows = ts + CONV_HALO - 8
        for s in range(1, 8):
            sh_ref[s - 1, 0:nrows, :] = ext_ref[pl.ds(s, nrows), lanes]
        for r0 in range(0, ts, rb):
            acc = jnp.broadcast_to(b_ref[:, lanes], (rb, LANES))
            for k in range(ksize):
                s, a8 = (base + k) % 8, 8 * ((base + k) // 8)
                if s == 0:
                    win = ext_ref[pl.ds(r0 + a8, rb), lanes]
                else:
                    win = sh_ref[s - 1, pl.ds(r0 + a8, rb), :]
                acc = acc + w_ref[k:k + 1, lanes] * win
            y_ref[pl.ds(r0, rb), lanes] = acc
        return carry

    lax.fori_loop(0, c // LANES, chunk, 0)
    y = y_ref[...]
    mu = jnp.mean(y, axis=-1, keepdims=True)
    yc = y - mu
    var = jnp.mean(yc * yc, axis=-1, keepdims=True)
    z = yc * lax.rsqrt(var + LN_EPS) * g_ref[...] + beta_ref[...]
    o_ref[...] = (z * _sigmoid(z)).astype(o_ref.dtype)


def _dwconv_ln(c0, w, b, g, beta):
    bsz, seq, c = c0.shape
    ksize = w.shape[0]
    assert ksize - 1 <= CONV_HALO and c % LANES == 0
    ts = _row_tile(seq, 256)
    assert ts % CONV_HALO == 0
    r = ts // CONV_HALO
    return pl.pallas_call(
        functools.partial(_dwconv_ln_kernel, ksize=ksize),
        out_shape=jax.ShapeDtypeStruct((bsz, seq, c), BF16),
        grid=(bsz, seq // ts),
        in_specs=[pl.BlockSpec((None, ts, c), lambda bb, i: (bb, i, 0)),
                  pl.BlockSpec((None, CONV_HALO, c), lambda bb, i: (bb, jnp.maximum(i * r - 1, 0), 0)),
                  pl.BlockSpec((ksize, c), lambda bb, i: (0, 0)),
                  pl.BlockSpec((1, c), lambda bb, i: (0, 0)),
                  pl.BlockSpec((1, c), lambda bb, i: (0, 0)),
                  pl.BlockSpec((1, c), lambda bb, i: (0, 0))],
        out_specs=pl.BlockSpec((None, ts, c), lambda bb, i: (bb, i, 0)),
        scratch_shapes=[pltpu.VMEM((ts + CONV_HALO, c), F32), pltpu.VMEM((ts, c), F32),
                        pltpu.VMEM((7, ts + CONV_HALO - 8, LANES), F32)],
        compiler_params=_cparams(2, ts * c * 4 * 8),
        name="dwconv_ln",
    )(c0, c0, w, b.reshape(1, c), g.reshape(1, c), beta.reshape(1, c))


def _zero_fill(refs):
    rows = refs[0].shape[0]

    def body(i, carry):
        r = pl.multiple_of(i * 8, 8)
        for ref in refs:
            ref[pl.ds(r, 8), :] = jnp.zeros((8, ref.shape[1]), F32)
        return carry

    lax.fori_loop(0, rows // 8, body, 0)


def _pipelined_kernel(ti_ref, tj_ref, *refs, n_a, dot_a, e_tile, epilogue, row_chunk):
    n_d, n_e = len(dot_a), len(e_tile)
    a_refs, b_refs = refs[:n_a], refs[n_a:n_a + n_d]
    e_refs = refs[n_a + n_d:n_a + n_d + n_e]
    o_ref = refs[n_a + n_d + n_e]
    z_even, z_odd = refs[-2 * n_d:-n_d], refs[-n_d:]
    t = pl.program_id(0)
    tm = o_ref.shape[0]

    @pl.when(t == 0)
    def _():
        _zero_fill(z_odd)

    def step(zr, zw):
        def epilogue_rows(r0, r1):
            for r in range(r0, r1, row_chunk):
                rs = slice(r, r + row_chunk)
                es = [e[rs, :] if tile else e[...] for e, tile in zip(e_refs, e_tile)]
                o_ref[rs, :] = epilogue([z[rs, :] for z in zr], es).astype(o_ref.dtype)

        chunks = [(d, kk) for d in range(n_d)
                  for kk in range(a_refs[dot_a[d]].shape[1] // _tile(a_refs[dot_a[d]].shape[1], MXU_K))]
        bounds = [(tm * s // len(chunks)) // row_chunk * row_chunk for s in range(len(chunks) + 1)]
        acc = None
        for s, (d, kk) in enumerate(chunks):
            a_ref, b_ref = a_refs[dot_a[d]], b_refs[d]
            kc = _tile(a_ref.shape[1], MXU_K)
            part = _dot(a_ref[:, kk * kc:(kk + 1) * kc], b_ref[kk * kc:(kk + 1) * kc, :])
            acc = part if kk == 0 else acc + part
            epilogue_rows(bounds[s], bounds[s + 1])
            if (kk + 1) * kc == a_ref.shape[1]:
                zw[d][...] = acc

    parity = lax.rem(t, 2)

    @pl.when(parity == 0)
    def _():
        step(z_odd, z_even)

    @pl.when(parity == 1)
    def _():
        step(z_even, z_odd)


def _pipelined_matmul(name, m, n, tm, tn, a_ops, b_ops, e_ops, out_dtype, epilogue, row_chunk=32):
    ni, nj = m // tm, n // tn
    steps = ni * nj
    order = [min(max(s - 1, 0), steps - 1) for s in range(steps + 2)]
    ti = jnp.asarray([s // nj for s in order], jnp.int32)
    tj = jnp.asarray([s % nj for s in order], jnp.int32)
    specs, args = [], []
    vmem = 2 * tm * tn * jnp.dtype(out_dtype).itemsize + 2 * len(b_ops) * tm * tn * 4
    for a in a_ops:
        specs.append(pl.BlockSpec((tm, a.shape[1]), lambda t, ti, tj: (ti[t + 1], 0)))
        args.append(a)
        vmem += 2 * tm * a.shape[1] * 2
    for w, layer, off, _ in b_ops:
        assert off % tn == 0
        specs.append(pl.BlockSpec((None, w.shape[1], tn),
                                  lambda t, ti, tj, layer=layer, o=off // tn: (layer, 0, tj[t + 1] + o)))
        args.append(w)
        vmem += 2 * w.shape[1] * tn * 2
    e_tile = []
    for e, off in e_ops:
        assert off % tn == 0
        tile = e.shape[0] != 1
        e_tile.append(tile)
        if tile:
            specs.append(pl.BlockSpec((tm, tn), lambda t, ti, tj, o=off // tn: (ti[t], tj[t] + o)))
        else:
            specs.append(pl.BlockSpec((1, tn), lambda t, ti, tj, o=off // tn: (0, tj[t] + o)))
        args.append(e)
        vmem += 2 * (tm if tile else 8) * tn * e.dtype.itemsize
    zbuf = pltpu.VMEM((tm, tn), F32)
    return pl.pallas_call(
        functools.partial(_pipelined_kernel, n_a=len(a_ops), dot_a=tuple(b[3] for b in b_ops),
                          e_tile=tuple(e_tile), epilogue=epilogue, row_chunk=min(tm, row_chunk)),
        out_shape=jax.ShapeDtypeStruct((m, n), out_dtype),
        grid_spec=pltpu.PrefetchScalarGridSpec(
            num_scalar_prefetch=2,
            grid=(steps + 1,),
            in_specs=specs,
            out_specs=pl.BlockSpec((tm, tn), lambda t, ti, tj: (ti[t], tj[t])),
            scratch_shapes=[zbuf] * (2 * len(b_ops))),
        compiler_params=_cparams(1, vmem),
        name=name,
    )(ti, tj, *args)


def _merge_epilogue(zs, es):
    y = es[0].astype(F32) * zs[0]
    y = y + es[1].astype(F32) * zs[1]
    return y + es[2].astype(F32) * zs[2]


def _merge(a, c, xo, wa, wc, wx, gates, layer, d):
    m = a.shape[0]
    tm, tn = _row_tile(m, 1024), _tile(d, 256)
    return _pipelined_matmul("merge", m, d, tm, tn, [a, c, xo],
                             [(wa, layer, 0, 0), (wc, layer, 0, 1), (wx, layer, 0, 2)],
                             [(gates, 0), (gates, d), (gates, 2 * d)], BF16, _merge_epilogue)


def _gate_epilogue(zs, es):
    return _sigmoid(zs[0] + es[0])


def _ffn_up_kernel(ti_ref, tj_ref, a_ref, b1_ref, b2_ref, w1_ref, w2_ref, c1_ref, c2_ref, o_ref,
                   za1, za2, zb1, zb2, carry1, carry2, *, ksize, tiles_per_seq, row_chunk):
    t = pl.program_id(0)
    tm = a_ref.shape[0]
    halo = FFN_HALO

    @pl.when(t == 0)
    def _():
        def zero_rows(i, carry):
            r = pl.multiple_of(i * halo, halo)
            zb1[pl.ds(r, halo), :] = jnp.zeros((halo, zb1.shape[1]), F32)
            zb2[pl.ds(r, halo), :] = jnp.zeros((halo, zb2.shape[1]), F32)
            return carry

        lax.fori_loop(0, zb1.shape[0] // halo, zero_rows, 0)

    def step(zr1, zr2, zw1, zw2):
        i_prev, j_prev = ti_ref[t], tj_ref[t]
        seq_start = lax.rem(i_prev, tiles_per_seq) == 0

        @pl.when(seq_start)
        def _():
            zr1[0:halo, :] = jnp.zeros((halo, zr1.shape[1]), F32)
            zr2[0:halo, :] = jnp.zeros((halo, zr2.shape[1]), F32)
            carry1[j_prev] = zr1[tm:tm + halo, :]
            carry2[j_prev] = zr2[tm:tm + halo, :]

        @pl.when(jnp.logical_not(seq_start))
        def _():
            zr1[0:halo, :] = carry1[j_prev]
            zr2[0:halo, :] = carry2[j_prev]
            carry1[j_prev] = zr1[tm:tm + halo, :]
            carry2[j_prev] = zr2[tm:tm + halo, :]

        def conv(zr, w_ref, c_ref, r):
            acc = c_ref[...] + w_ref[ksize - 1:ksize, :] * zr[halo + r:halo + r + row_chunk, :]
            for k in range(ksize - 1):
                shift = ksize - 1 - k
                acc = acc + w_ref[k:k + 1, :] * zr[pl.ds(halo + r - shift, row_chunk), :]
            return acc

        def epilogue_rows(r0, r1):
            for r in range(r0, r1, row_chunk):
                u1 = conv(zr1, w1_ref, c1_ref, r)
                u2 = conv(zr2, w2_ref, c2_ref, r)
                o_ref[r:r + row_chunk, :] = (u1 * _sigmoid(u1) * u2).astype(o_ref.dtype)

        kdim = a_ref.shape[1]
        kc = _tile(kdim, MXU_K)
        nk = kdim // kc
        bounds = [(tm * s // (2 * nk)) // row_chunk * row_chunk for s in range(2 * nk + 1)]
        slot = 0
        for zw, b_ref in ((zw1, b1_ref), (zw2, b2_ref)):
            acc = None
            for kk in range(nk):
                part = _dot(a_ref[:, kk * kc:(kk + 1) * kc], b_ref[kk * kc:(kk + 1) * kc, :])
                acc = part if acc is None else acc + part
                epilogue_rows(bounds[slot], bounds[slot + 1])
                slot += 1
            zw[halo:, :] = acc

    parity = lax.rem(t, 2)

    @pl.when(parity == 0)
    def _():
        step(zb1, zb2, za1, za2)

    @pl.when(parity == 1)
    def _():
        step(za1, za2, zb1, zb2)


def _ffn_up(h, w_up, layer, conv_w, conv_b, dff, seq):
    m, k = h.shape
    ksize = conv_w.shape[0]
    assert ksize - 1 <= FFN_HALO
    tm = _row_tile(seq, 1024)
    tn = _tile(dff, 512)
    ni, nf = m // tm, dff // tn
    steps = ni * nf
    order = [min(max(s - 1, 0), steps - 1) for s in range(steps + 2)]
    ti = jnp.asarray([s // nf for s in order], jnp.int32)
    tj = jnp.asarray([s % nf for s in order], jnp.int32)
    vmem = 2 * (tm * k * 2 + 2 * k * tn * 2 + tm * tn * 2) + 4 * (tm + FFN_HALO) * tn * 4 + 2 * nf * FFN_HALO * tn * 4

    def wspec(rows, which):
        return pl.BlockSpec((rows, tn), lambda t, ti, tj: (0, tj[t] + which * nf))

    def bspec(which):
        return pl.BlockSpec((None, k, tn), lambda t, ti, tj: (layer, 0, tj[t + 1] + which * nf))

    zbuf = pltpu.VMEM((tm + FFN_HALO, tn), F32)
    carry = pltpu.VMEM((nf, FFN_HALO, tn), F32)
    two_f = 2 * dff
    return pl.pallas_call(
        functools.partial(_ffn_up_kernel, ksize=ksize, tiles_per_seq=seq // tm, row_chunk=min(tm, 32)),
        out_shape=jax.ShapeDtypeStruct((m, dff), BF16),
        grid_spec=pltpu.PrefetchScalarGridSpec(
            num_scalar_prefetch=2,
            grid=(steps + 1,),
            in_specs=[pl.BlockSpec((tm, k), lambda t, ti, tj: (ti[t + 1], 0)),
                      bspec(0), bspec(1), wspec(ksize, 0), wspec(ksize, 1), wspec(1, 0), wspec(1, 1)],
            out_specs=pl.BlockSpec((tm, tn), lambda t, ti, tj: (ti[t], tj[t])),
            scratch_shapes=[zbuf, zbuf, zbuf, zbuf, carry, carry]),
        compiler_params=_cparams(1, vmem),
        name="ffn_up",
    )(ti, tj, h, w_up, w_up, conv_w, conv_w, conv_b.reshape(1, two_f), conv_b.reshape(1, two_f))


def kernel(x, mem, attn_norm, w_in, swa_q_norm, swa_k_norm, swa_sinks, conv_dw_w, conv_dw_b, conv_ln_g, conv_ln_b,
           mem_norm, w_mem_kv, x_q_norm, x_k_norm, gate_b, w_proj_a, w_proj_c, w_proj_x, w_o, ffn_norm, w_up,
           ffn_dw_w, ffn_dw_b, w_down):
    bsz, seq, d = x.shape
    mlen = mem.shape[1]
    depth = w_in.shape[0]
    hq, hd = swa_sinks.shape[1], swa_q_norm.shape[1]
    swa_q = w_proj_a.shape[1]
    conv_w = conv_dw_w.shape[2]
    x_q = w_mem_kv.shape[2] // 2
    xhd = x_q_norm.shape[1]
    dff = w_down.shape[1]
    swa_kv = (w_in.shape[2] - swa_q - 2 * conv_w - x_q - 3 * d) // 2
    hkv = swa_kv // hd
    assert swa_q == hq * hd and hkv * hd == swa_kv
    off_glu = swa_q + 2 * swa_kv
    off_xq = off_glu + 2 * conv_w
    off_gate = off_xq + x_q
    m = bsz * seq

    w_in_b, w_mem_b = w_in.astype(BF16), w_mem_kv.astype(BF16)
    wa_b, wc_b, wx_b = w_proj_a.astype(BF16), w_proj_c.astype(BF16), w_proj_x.astype(BF16)
    wo_b, wup_b, wdn_b = w_o.astype(BF16), w_up.astype(BF16), w_down.astype(BF16)

    xf = x.reshape(m, d)
    memf = mem.reshape(bsz * mlen, d)
    for l in range(depth):
        h = _rmsnorm(xf, attn_norm[l])
        qkv = _matmul(h, w_in_b, l, 0, off_glu, out_dtype=F32, tn_pref=512, name="inproj_qkv")
        c0 = _matmul(h, w_in_b, l, off_glu, conv_w, off2=off_glu + conv_w, kind="glu", out_dtype=F32,
                     tn_pref=512, name="inproj_glu")
        gates = _pipelined_matmul("inproj_gates", m, 3 * d, _row_tile(m, 1024), _tile(3 * d, 512, off_gate), [h],
                                  [(w_in_b, l, off_gate, 0)], [(gate_b[l].reshape(1, 3 * d), 0)], BF16,
                                  _gate_epilogue)
        mn = _rmsnorm(memf, mem_norm[l])
        mk = _matmul(mn, w_mem_b, l, 0, x_q, kind="headnorm", extra=x_k_norm[l], out_dtype=BF16, name="mem_k")
        mv = _matmul(mn, w_mem_b, l, x_q, x_q, out_dtype=BF16, tn_pref=512, name="mem_v")
        xo = _xattn(h, w_in_b, l, off_xq, x_q_norm[l], mk.reshape(bsz, mlen, x_q), mv.reshape(bsz, mlen, x_q), seq)
        a = _swa(qkv.reshape(bsz, seq, off_glu), swa_sinks[l], swa_q_norm[l], swa_k_norm[l], hq=hq, hkv=hkv, hd=hd)
        c = _dwconv_ln(c0.reshape(bsz, seq, conv_w), conv_dw_w[l], conv_dw_b[l], conv_ln_g[l], conv_ln_b[l])
        y = _merge(a.reshape(m, swa_q), c.reshape(m, conv_w), xo, wa_b, wc_b, wx_b, gates, l, d)
        xf = _matmul(y, wo_b, l, 0, d, kind="resid", extra=xf, out_dtype=F32, name="out_proj")
        h2 = _rmsnorm(xf, ffn_norm[l])
        act = _ffn_up(h2, wup_b, l, ffn_dw_w[l], ffn_dw_b[l], dff, seq)
        xf = _matmul(act, wdn_b, l, 0, d, kind="resid", extra=xf, out_dtype=F32, tn_pref=512, name="ffn_down")
    return xf.reshape(bsz, seq, d)
```
